```python
import jax, jax.numpy as jnp
from jax import lax
import numpy as np

D_MODEL = 1024
BATCH = 16
SEQ = 2048
DEPTH = 1

N_META = 16
EPS = 1e-6
MLA_HEADS = 8
Q_LORA_RANK = 384
KV_LORA_RANK = 256
QK_NOPE_DIM = 128
QK_ROPE_DIM = 64
V_HEAD_DIM = 128
MLA_WIDTH = MLA_HEADS * V_HEAD_DIM
ROPE_THETA = 10000.0
Q_BLOCK = 128
HGRN_KEY_DIM = 128
HGRN_HEADS = D_MODEL // HGRN_KEY_DIM
HGRN_VAL_DIM = D_MODEL // HGRN_HEADS
HGRN_K_WIDTH = HGRN_HEADS * HGRN_KEY_DIM
HGRN_V_WIDTH = HGRN_HEADS * HGRN_VAL_DIM
CHUNK = 64
SUB_CHUNK = 16
N_SUB = CHUNK // SUB_CHUNK
HGRN_PAD = CHUNK - N_META
D_FF = -(-8 * D_MODEL // (3 * 256)) * 256
IN_SIZES = [Q_LORA_RANK, KV_LORA_RANK, QK_ROPE_DIM,
            HGRN_K_WIDTH, HGRN_K_WIDTH, HGRN_K_WIDTH, HGRN_V_WIDTH, HGRN_V_WIDTH,
            D_MODEL, D_MODEL]
IN_WIDTH = int(sum(IN_SIZES))
IN_SPLIT = [int(v) for v in np.cumsum(IN_SIZES)[:-1]]

kernel_name = "hybrid_mla_hgrn2_gated_encoder"


def rms_norm(x, gain):
    xf = x.astype(jnp.float32)
    y = xf * lax.rsqrt(jnp.mean(xf * xf, axis=-1, keepdims=True) + EPS)
    return (y * gain.astype(jnp.float32)).astype(x.dtype)


def rope_tables(length):
    inv = ROPE_THETA ** (-jnp.arange(0, QK_ROPE_DIM, 2, dtype=jnp.float32) / QK_ROPE_DIM)
    ang = jnp.arange(length, dtype=jnp.float32)[:, None] * inv[None, :]
    return jnp.cos(ang), jnp.sin(ang)


def apply_rope(x, cos, sin):
    xf = x.astype(jnp.float32)
    x1, x2 = jnp.split(xf, 2, axis=-1)
    return jnp.concatenate([x1 * cos - x2 * sin, x2 * cos + x1 * sin], axis=-1).astype(x.dtype)


def mla(c_q, c_kv, k_rope_raw, g_cq, g_ckv, w_uq, w_ukv, cos, sin):
    B, L, _ = c_q.shape
    q = (rms_norm(c_q, g_cq) @ w_uq).reshape(B, L, MLA_HEADS, QK_NOPE_DIM + QK_ROPE_DIM)
    q_nope, q_rope = q[..., :QK_NOPE_DIM], q[..., QK_NOPE_DIM:]
    q_rope = apply_rope(q_rope, cos[:, None, :], sin[:, None, :])
    kv = (rms_norm(c_kv, g_ckv) @ w_ukv).reshape(B, L, MLA_HEADS, QK_NOPE_DIM + V_HEAD_DIM)
    k_nope, v = kv[..., :QK_NOPE_DIM], kv[..., QK_NOPE_DIM:]
    k_rope = apply_rope(k_rope_raw, cos, sin)
    scale = (QK_NOPE_DIM + QK_ROPE_DIM) ** -0.5

    def attend(qn, qr):
        s = (jnp.einsum('bqhd,bkhd->bhqk', qn, k_nope)
             + jnp.einsum('bqhr,bkr->bhqk', qr, k_rope)).astype(jnp.float32) * scale
        p = jax.nn.softmax(s, axis=-1).astype(v.dtype)
        return jnp.einsum('bhqk,bkhd->bqhd', p, v)

    o_meta = attend(q_nope[:, :N_META], q_rope[:, :N_META])
    n_blk = (L - N_META) // Q_BLOCK

    def blocks(t):
        return jnp.moveaxis(t[:, N_META:].reshape(B, n_blk, Q_BLOCK, *t.shape[2:]), 1, 0)

    o_real = lax.map(lambda a: attend(a[0], a[1]), (blocks(q_nope), blocks(q_rope)))
    o_real = jnp.moveaxis(o_real, 0, 1).reshape(B, L - N_META, MLA_HEADS, V_HEAD_DIM)
    return jnp.concatenate([o_meta, o_real], axis=1).reshape(B, L, MLA_WIDTH)


def gla_chunkwise(q, k, v, g):
    B, H, T, dk = q.shape
    dv = v.shape[-1]
    n = T // CHUNK

    def chunks(t):
        return jnp.moveaxis(t.reshape(B, H, n, CHUNK, t.shape[-1]), 2, 0)

    valid = (jnp.arange(CHUNK)[None, :] < (jnp.arange(N_SUB)[:, None] + 1) * SUB_CHUNK)[:, :, None]
    causal = jnp.tril(jnp.ones((CHUNK, CHUNK), dtype=bool))

    def step(S, inp):
        q_c, k_c, v_c, g_c = inp
        b = jnp.cumsum(g_c, axis=2)
        bs = b.reshape(B, H, N_SUB, SUB_CHUNK, dk)
        gs = g_c.reshape(B, H, N_SUB, SUB_CHUNK, dk)
        b_ref = bs[:, :, :, 0] - gs[:, :, :, 0]
        q_r = q_c.reshape(B, H, N_SUB, SUB_CHUNK, dk) * jnp.exp(bs - b_ref[:, :, :, None])
        expo = b_ref[:, :, :, None, :] - b[:, :, None, :, :]
        k_r = jnp.where(valid, k_c[:, :, None] * jnp.exp(jnp.where(valid, expo, 0.0)), 0.0)
        a = jnp.einsum('bhitd,bhisd->bhits', q_r, k_r).reshape(B, H, CHUNK, CHUNK)
        a = jnp.where(causal, a, 0.0)
        o = jnp.einsum('bhts,bhsv->bhtv', a, v_c) + jnp.einsum('bhtd,bhdv->bhtv', q_c * jnp.exp(b), S)
        b_last = b[:, :, -1:]
        S = jnp.exp(b_last)[:, :, 0, :, None] * S + jnp.einsum(
            'bhcd,bhcv->bhdv', k_c * jnp.exp(b_last - b), v_c)
        return S, o

    S0 = jnp.zeros((B, H, dk, dv), jnp.float32)
    _, o = lax.scan(step, S0, (chunks(q), chunks(k), chunks(v), chunks(g)))
    return jnp.moveaxis(o, 0, 2).reshape(B, H, T, dv)


def hgrn2(q_raw, f_fwd, f_bwd, i_raw, g_raw, lb, g_norm):
    B, L, _ = q_raw.shape
    pad = ((0, 0), (0, 0), (HGRN_PAD, 0), (0, 0))

    def heads(t, d):
        return jnp.pad(t.reshape(B, L, HGRN_HEADS, d).transpose(0, 2, 1, 3), pad)

    q = heads(jax.nn.silu(q_raw), HGRN_KEY_DIM)
    v = heads(i_raw, HGRN_VAL_DIM)

    def direction(f_logit, lb_d, flip):
        f = lb_d + (1.0 - lb_d) * jax.nn.sigmoid(f_logit.astype(jnp.float32))
        k = heads((1.0 - f).astype(q.dtype), HGRN_KEY_DIM)
        g = heads(jnp.log(f), HGRN_KEY_DIM)
        if flip:
            rev = lambda t: jnp.flip(t, axis=2)
            return rev(gla_chunkwise(rev(q), rev(k), rev(v), rev(g)))
        return gla_chunkwise(q, k, v, g)

    o = direction(f_fwd, lb[0], False) + direction(f_bwd, lb[1], True)
    o = rms_norm(o[:, :, HGRN_PAD:], g_norm)
    o = o.transpose(0, 2, 1, 3).reshape(B, L, HGRN_V_WIDTH).astype(q_raw.dtype)
    return o * jax.nn.silu(g_raw)


def setup_inputs(seed: int = 0) -> dict:
    key = jax.random.key(seed)
    ks = jax.random.split(key, 20)
    f32 = jnp.float32
    nrm = lambda k, shape, fan_in: jax.random.normal(k, shape, f32) * fan_in ** -0.5
    gain = lambda k, shape: 1.0 + 0.02 * jax.random.normal(k, shape, f32)
    return {
        "x": jax.random.normal(ks[0], (BATCH, SEQ, D_MODEL), f32),
        "meta_tokens": jax.random.normal(ks[1], (N_META, D_MODEL), f32),
        "g_mix_pre": gain(ks[2], (DEPTH, D_MODEL)),
        "w_in": nrm(ks[3], (DEPTH, D_MODEL, IN_WIDTH), D_MODEL),
        "g_cq": gain(ks[4], (DEPTH, Q_LORA_RANK)),
        "g_ckv": gain(ks[5], (DEPTH, KV_LORA_RANK)),
        "w_uq": nrm(ks[6], (DEPTH, Q_LORA_RANK, MLA_HEADS * (QK_NOPE_DIM + QK_ROPE_DIM)), Q_LORA_RANK),
        "w_ukv": nrm(ks[7], (DEPTH, KV_LORA_RANK, MLA_HEADS * (QK_NOPE_DIM + V_HEAD_DIM)), KV_LORA_RANK),
        "lb_logits": 0.1 * jax.random.normal(ks[8], (DEPTH + 1, 2, HGRN_K_WIDTH), f32),
        "g_hgrn": gain(ks[9], (DEPTH, HGRN_VAL_DIM)),
        "w_o": nrm(ks[10], (DEPTH, D_MODEL, D_MODEL), D_MODEL),
        "g_mix_post": gain(ks[11], (DEPTH, D_MODEL)),
        "g_ffn_pre": gain(ks[12], (DEPTH, D_MODEL)),
        "w_gate": nrm(ks[13], (DEPTH, D_MODEL, D_FF), D_MODEL),
        "w_up": nrm(ks[14], (DEPTH, D_MODEL, D_FF), D_MODEL),
        "w_down": nrm(ks[15], (DEPTH, D_FF, D_MODEL), D_FF),
        "g_ffn_post": gain(ks[16], (DEPTH, D_MODEL)),
    }


def reference(x, meta_tokens, g_mix_pre, w_in, g_cq, g_ckv, w_uq, w_ukv, lb_logits, g_hgrn,
              w_o, g_mix_post, g_ffn_pre, w_gate, w_up, w_down, g_ffn_post):
    B = x.shape[0]
    meta = jnp.broadcast_to(meta_tokens.astype(x.dtype)[None], (B, N_META, D_MODEL))
    h = jnp.concatenate([meta, x], axis=1)
    L = h.shape[1]
    cos, sin = rope_tables(L)
    lb_all = jnp.cumsum(jax.nn.softmax(lb_logits.astype(jnp.float32), axis=0), axis=0)
    for l in range(DEPTH):
        hn = rms_norm(h, g_mix_pre[l])
        proj = hn @ w_in[l]
        c_q, c_kv, k_rope, hq, hf_fwd, hf_bwd, hi, hg, gate_a, gate_h = jnp.split(proj, IN_SPLIT, axis=-1)
        a = mla(c_q, c_kv, k_rope, g_cq[l], g_ckv[l], w_uq[l], w_ukv[l], cos, sin)
        r = hgrn2(hq, hf_fwd, hf_bwd, hi, hg, lb_all[l], g_hgrn[l])
        mixed = jax.nn.sigmoid(gate_a) * a + jax.nn.sigmoid(gate_h) * r
        h = h + rms_norm(mixed @ w_o[l], g_mix_post[l])
        hn = rms_norm(h, g_ffn_pre[l])
        ff = (jax.nn.silu(hn @ w_gate[l]) * (hn @ w_up[l])) @ w_down[l]
        h = h + rms_norm(ff, g_ffn_post[l])
    return h[:, N_META:]
```

```python
import functools

import numpy as np
import jax
import jax.numpy as jnp
from jax import lax
from jax.experimental import pallas as pl
from jax.experimental.pallas import tpu as pltpu

F32 = jnp.float32
BF16 = jnp.bfloat16

N_META = 16
EPS = 1e-6
D_MODEL = 1024
HEADS = 8
Q_LORA = 384
KV_LORA = 256
NOPE = 128
ROPE = 64
V_DIM = 128
ROPE_THETA = 10000.0
HEAD_DIM = 128
CHUNK = 64
SUB = 16
N_SUB = CHUNK // SUB

LANE = 128
VMEM_LIMIT_BYTES = 56 * 1024 * 1024

LAT_WIDTH = Q_LORA + KV_LORA + LANE
N_BIG = 7
NEG_BIG = -1e30


def _resident(shape):
    zeros = (0,) * len(shape)
    return pl.BlockSpec(shape, lambda *_: zeros, pipeline_mode=pl.Buffered(1))


def _rms(xf, gain):
    return xf * lax.rsqrt(jnp.mean(xf * xf, axis=-1, keepdims=True) + EPS) * gain


def _sigmoid(z):
    return 1.0 / (1.0 + jnp.exp(-z))


def _dot(a, b):
    return jnp.dot(a, b, preferred_element_type=F32)


def _dot_nt(a, b):
    return lax.dot_general(a, b, (((1,), (1,)), ((), ())), preferred_element_type=F32)


def _dot_tn(a, b):
    return lax.dot_general(a, b, (((0,), (0,)), ((), ())), preferred_element_type=F32)


def _chunk_cumsum(g, chunk, reverse):
    rows = g.shape[0]
    pos = lax.broadcasted_iota(jnp.int32, g.shape, 0) % chunk
    step = 1
    while step < chunk:
        if reverse:
            shifted = pltpu.roll(g, rows - step, axis=0)
            ok = pos < chunk - step
        else:
            shifted = pltpu.roll(g, step, axis=0)
            ok = pos >= step
        g = g + jnp.where(ok, shifted, 0.0)
        step *= 2
    return g


def _in_proj_kernel(x_ref, gpre_ref, wbig_ref, wlat_ref, lb_ref,
                    q_ref, kf_ref, kb_ref, bf_ref, cb_ref, v_ref, sg_ref, ga_ref, gh_ref, lat_ref,
                    *, chunk):
    hn = _rms(x_ref[...], gpre_ref[...]).astype(BF16)

    def seg(j):
        return _dot(hn, wbig_ref[:, j * D_MODEL:(j + 1) * D_MODEL])

    z = seg(0)
    q_ref[...] = (z * _sigmoid(z)).astype(BF16)
    for d, (k_ref, s_ref) in enumerate(((kf_ref, bf_ref), (kb_ref, cb_ref))):
        lb = lb_ref[d:d + 1, :]
        f = lb + (1.0 - lb) * _sigmoid(seg(1 + d))
        k_ref[...] = (1.0 - f).astype(BF16)
        s_ref[...] = _chunk_cumsum(jnp.log(f), chunk, reverse=(d == 1))
    v_ref[...] = seg(3).astype(BF16)
    z = seg(4)
    sg_ref[...] = (z * _sigmoid(z)).astype(BF16)
    ga_ref[...] = _sigmoid(seg(5)).astype(BF16)
    gh_ref[...] = _sigmoid(seg(6)).astype(BF16)
    lat_ref[...] = _dot(hn, wlat_ref[...]).astype(BF16)


def _in_proj(rows, gpre, wbig, wlat, lb, *, tm, chunk):
    n = rows.shape[0]
    tile = lambda w: pl.BlockSpec((tm, w), lambda i: (i, 0))
    wide = lambda dt: jax.ShapeDtypeStruct((n, D_MODEL), dt)
    out_dtypes = (BF16, BF16, BF16, F32, F32, BF16, BF16, BF16, BF16)
    return pl.pallas_call(
        functools.partial(_in_proj_kernel, chunk=chunk),
        grid=(n // tm,),
        in_specs=[tile(D_MODEL), _resident(gpre.shape), _resident(wbig.shape),
                  _resident(wlat.shape), _resident(lb.shape)],
        out_specs=[tile(D_MODEL)] * 9 + [tile(LAT_WIDTH)],
        out_shape=[wide(dt) for dt in out_dtypes] + [jax.ShapeDtypeStruct((n, LAT_WIDTH), BF16)],
        compiler_params=pltpu.CompilerParams(
            dimension_semantics=("parallel",), vmem_limit_bytes=VMEM_LIMIT_BYTES),
        name="in_proj",
    )(rows, gpre, wbig, wlat, lb)


def _rope(xr, cos, sin):
    return xr * cos + pltpu.roll(xr, LANE // 2, axis=1) * sin


def _mla_kernel(lat_ref, latm_ref, gcq_ref, gckv_ref, wuq_ref, wukv_ref,
                cos_ref, sin_ref, cosm_ref, sinm_ref,
                a_ref, k_s, v_s, km_s, vm_s, *, tq, kv_rows):
    qi = pl.program_id(1)
    seq = lat_ref.shape[1]
    scale = (NOPE + ROPE) ** -0.5
    kw = NOPE + LANE

    def keys_values(ckv, kr_raw, cos, sin):
        ckvn = _rms(ckv.astype(F32), gckv_ref[...]).astype(BF16)
        kv = _dot(ckvn, wukv_ref[...])
        kr = _rope(kr_raw.astype(F32), cos, sin).astype(BF16)
        return kv, kr

    @pl.when(qi == 0)
    def _():
        km_s[...] = jnp.zeros(km_s.shape, BF16)
        vm_s[...] = jnp.zeros(vm_s.shape, BF16)
        kv, kr = keys_values(latm_ref[:, Q_LORA:Q_LORA + KV_LORA], latm_ref[:, Q_LORA + KV_LORA:],
                             cosm_ref[...], sinm_ref[...])
        for h in range(HEADS):
            km_s[h, 0:N_META, 0:NOPE] = kv[:, h * NOPE:(h + 1) * NOPE].astype(BF16)
            km_s[h, 0:N_META, NOPE:kw] = kr
            vm_s[h, 0:N_META, :] = kv[:, HEADS * NOPE + h * V_DIM:HEADS * NOPE + (h + 1) * V_DIM].astype(BF16)

        def body(r, carry):
            rows = pl.ds(pl.multiple_of(r * kv_rows, kv_rows), kv_rows)
            kv, kr = keys_values(lat_ref[0, rows, Q_LORA:Q_LORA + KV_LORA],
                                 lat_ref[0, rows, Q_LORA + KV_LORA:],
                                 cos_ref[rows, :], sin_ref[rows, :])
            for h in range(HEADS):
                k_s[h, rows, 0:NOPE] = kv[:, h * NOPE:(h + 1) * NOPE].astype(BF16)
                k_s[h, rows, NOPE:kw] = kr
                v_s[h, rows, :] = kv[:, HEADS * NOPE + h * V_DIM:HEADS * NOPE + (h + 1) * V_DIM].astype(BF16)
            return carry

        lax.fori_loop(0, seq // kv_rows, body, 0)

    rows = pl.ds(pl.multiple_of(qi * tq, tq), tq)
    cqn = _rms(lat_ref[0, rows, 0:Q_LORA].astype(F32), gcq_ref[...]).astype(BF16)
    qall = _dot(cqn, wuq_ref[...])
    cos = cos_ref[rows, :]
    sin = sin_ref[rows, :]
    meta_valid = lax.broadcasted_iota(jnp.int32, (tq, LANE), 1) < N_META
    for h in range(HEADS):
        qn = qall[:, h * kw:h * kw + NOPE] * scale
        qr = _rope(qall[:, h * kw + NOPE:(h + 1) * kw], cos, sin) * scale
        qh = jnp.concatenate([qn, qr], axis=1).astype(BF16)
        s = _dot_nt(qh, k_s[h])
        sm = jnp.where(meta_valid, _dot_nt(qh, km_s[h]), NEG_BIG)
        m = jnp.maximum(jnp.max(s, axis=-1, keepdims=True), jnp.max(sm, axis=-1, keepdims=True))
        p = jnp.exp(s - m)
        pm = jnp.exp(sm - m)
        denom = jnp.sum(p, axis=-1, keepdims=True) + jnp.sum(pm, axis=-1, keepdims=True)
        o = _dot(p.astype(BF16), v_s[h]) + _dot(pm.astype(BF16), vm_s[h])
        a_ref[0, :, h * V_DIM:(h + 1) * V_DIM] = (o / denom).astype(BF16)


def _mla(lat, latm, gcq, gckv, wuq, wukv, cos, sin, cosm, sinm, *, tq, kv_rows):
    batch, seq, _ = lat.shape
    kw = NOPE + LANE
    return pl.pallas_call(
        functools.partial(_mla_kernel, tq=tq, kv_rows=kv_rows),
        grid=(batch, seq // tq),
        in_specs=[pl.BlockSpec((1, seq, LAT_WIDTH), lambda b, i: (b, 0, 0)),
                  _resident(latm.shape), _resident(gcq.shape), _resident(gckv.shape),
                  _resident(wuq.shape), _resident(wukv.shape),
                  _resident(cos.shape), _resident(sin.shape), _resident(cosm.shape), _resident(sinm.shape)],
        out_specs=pl.BlockSpec((1, tq, HEADS * V_DIM), lambda b, i: (b, i, 0)),
        out_shape=jax.ShapeDtypeStruct((batch, seq, HEADS * V_DIM), BF16),
        scratch_shapes=[pltpu.VMEM((HEADS, seq, kw), BF16), pltpu.VMEM((HEADS, seq, V_DIM), BF16),
                        pltpu.VMEM((HEADS, LANE, kw), BF16), pltpu.VMEM((HEADS, LANE, V_DIM), BF16)],
        compiler_params=pltpu.CompilerParams(
            dimension_semantics=("parallel", "arbitrary"), vmem_limit_bytes=VMEM_LIMIT_BYTES),
        name="mla",
    )(lat, latm, gcq, gckv, wuq, wukv, cos, sin, cosm, sinm)


def _gla_chunk(q, k, cum, v, st, reverse):
    qf = q.astype(F32)
    kf = k.astype(F32)
    row = lax.broadcasted_iota(jnp.int32, (CHUNK, HEAD_DIM), 0)
    zero = jnp.zeros((1, HEAD_DIM), F32)
    blocks = []
    for i in range(N_SUB):
        lo, hi = i * SUB, (i + 1) * SUB
        if reverse:
            ref = cum[hi:hi + 1, :] if i < N_SUB - 1 else zero
            valid = row >= lo
        else:
            ref = cum[lo - 1:lo, :] if i > 0 else zero
            valid = row < hi
        q_r = qf[lo:hi] * jnp.exp(cum[lo:hi] - ref)
        k_r = jnp.where(valid, kf * jnp.exp(jnp.where(valid, ref - cum, 0.0)), 0.0)
        blocks.append(_dot_nt(q_r.astype(BF16), k_r.astype(BF16)))
    r2 = lax.broadcasted_iota(jnp.int32, (CHUNK, CHUNK), 0)
    c2 = lax.broadcasted_iota(jnp.int32, (CHUNK, CHUNK), 1)
    keep = (r2 <= c2) if reverse else (r2 >= c2)
    a = jnp.where(keep, jnp.concatenate(blocks, axis=0), 0.0)
    total = cum[0:1, :] if reverse else cum[CHUNK - 1:CHUNK, :]
    o = _dot(a.astype(BF16), v) + _dot_nt((qf * jnp.exp(cum)).astype(BF16), st.astype(BF16))
    k_d = (kf * jnp.exp(total - cum)).astype(BF16)
    st_new = st * jnp.exp(total) + _dot_tn(v, k_d)
    return o, st_new


def _hgrn_kernel(q_ref, kf_ref, kb_ref, bf_ref, cb_ref, v_ref, sg_ref,
                 kfm_ref, bfm_ref, vm_ref, gn_ref, r_ref, acc_ref, st_ref, *, hg, norm_rows):
    seq = q_ref.shape[1]
    n_chunks = seq // CHUNK

    for h in range(hg):
        lanes = slice(h * HEAD_DIM, (h + 1) * HEAD_DIM)
        bm = bfm_ref[:, lanes]
        k_d = (kfm_ref[:, lanes].astype(F32) * jnp.exp(bm[N_META - 1:N_META, :] - bm)).astype(BF16)
        st_ref[0, h] = _dot_tn(vm_ref[:, lanes], k_d)
        st_ref[1, h] = jnp.zeros((HEAD_DIM, HEAD_DIM), F32)

    def sweep(c, accumulate):
        rf = pl.ds(pl.multiple_of(c * CHUNK, CHUNK), CHUNK)
        rb = pl.ds(pl.multiple_of((n_chunks - 1 - c) * CHUNK, CHUNK), CHUNK)
        for h in range(hg):
            lanes = slice(h * HEAD_DIM, (h + 1) * HEAD_DIM)
            for d, (rows, k_ref, cum_ref) in enumerate(((rf, kf_ref, bf_ref), (rb, kb_ref, cb_ref))):
                o, st_new = _gla_chunk(q_ref[0, rows, lanes], k_ref[0, rows, lanes], cum_ref[0, rows, lanes],
                                       v_ref[0, rows, lanes], st_ref[d, h], reverse=(d == 1))
                st_ref[d, h] = st_new
                if accumulate:
                    acc_ref[rows, lanes] += o
                else:
                    acc_ref[rows, lanes] = o

    def first(c, carry):
        sweep(c, accumulate=False)
        return carry

    def second(c, carry):
        sweep(c, accumulate=True)
        return carry

    lax.fori_loop(0, n_chunks // 2, first, 0)
    lax.fori_loop(n_chunks // 2, n_chunks, second, 0)

    def finish(i, carry):
        rows = pl.ds(pl.multiple_of(i * norm_rows, norm_rows), norm_rows)
        for h in range(hg):
            lanes = slice(h * HEAD_DIM, (h + 1) * HEAD_DIM)
            o = _rms(acc_ref[rows, lanes], gn_ref[...])
            r_ref[0, rows, lanes] = (o * sg_ref[0, rows, lanes].astype(F32)).astype(BF16)
        return carry

    lax.fori_loop(0, seq // norm_rows, finish, 0)


def _hgrn(q, kf, kb, bf, cb, v, sg, kfm, bfm, vm, gn, *, hg, norm_rows):
    batch, seq, width = q.shape
    gw = hg * HEAD_DIM
    blk = pl.BlockSpec((1, seq, gw), lambda b, g: (b, 0, g))
    meta = pl.BlockSpec((N_META, gw), lambda b, g: (0, g))
    return pl.pallas_call(
        functools.partial(_hgrn_kernel, hg=hg, norm_rows=norm_rows),
        grid=(batch, width // gw),
        in_specs=[blk] * 7 + [meta] * 3 + [_resident(gn.shape)],
        out_specs=blk,
        out_shape=jax.ShapeDtypeStruct((batch, seq, width), BF16),
        scratch_shapes=[pltpu.VMEM((seq, gw), F32), pltpu.VMEM((2, hg, HEAD_DIM, HEAD_DIM), F32)],
        compiler_params=pltpu.CompilerParams(
            dimension_semantics=("parallel", "parallel"), vmem_limit_bytes=VMEM_LIMIT_BYTES),
        name="hgrn",
    )(q, kf, kb, bf, cb, v, sg, kfm, bfm, vm, gn)


def _post_kernel(x_ref, a_ref, r_ref, ga_ref, gh_ref, wo_ref, gpost_ref, gfpre_ref,
                 wg_ref, wu_ref, wd_ref, gfpost_ref, out_ref, *, ff_chunk):
    mixed = (ga_ref[...].astype(F32) * a_ref[...].astype(F32)
             + gh_ref[...].astype(F32) * r_ref[...].astype(F32)).astype(BF16)
    h1 = x_ref[...] + _rms(_dot(mixed, wo_ref[...]), gpost_ref[...])
    hn = _rms(h1, gfpre_ref[...]).astype(BF16)
    d_ff = wg_ref.shape[1]
    ff = jnp.zeros(h1.shape, F32)
    for c in range(d_ff // ff_chunk):
        cols = slice(c * ff_chunk, (c + 1) * ff_chunk)
        gate = _dot(hn, wg_ref[:, cols])
        act = (gate * _sigmoid(gate) * _dot(hn, wu_ref[:, cols])).astype(BF16)
        ff = ff + _dot(act, wd_ref[cols, :])
    out_ref[...] = h1 + _rms(ff, gfpost_ref[...])


def _post(x, a, r, ga, gh, wo, gpost, gfpre, wg, wu, wd, gfpost, *, tm, ff_chunk):
    n = x.shape[0]
    tile = pl.BlockSpec((tm, D_MODEL), lambda i: (i, 0))
    return pl.pallas_call(
        functools.partial(_post_kernel, ff_chunk=ff_chunk),
        grid=(n // tm,),
        in_specs=[tile] * 5 + [_resident(w.shape) for w in (wo, gpost, gfpre, wg, wu, wd, gfpost)],
        out_specs=tile,
        out_shape=jax.ShapeDtypeStruct((n, D_MODEL), F32),
        compiler_params=pltpu.CompilerParams(
            dimension_semantics=("parallel",), vmem_limit_bytes=VMEM_LIMIT_BYTES),
        name="post",
    )(x, a, r, ga, gh, wo, gpost, gfpre, wg, wu, wd, gfpost)


def _rope_lane_map():
    half = ROPE // 2
    return np.concatenate([np.arange(half), LANE // 2 + np.arange(half)])


def _rope_tables(length):
    half = ROPE // 2
    inv = ROPE_THETA ** (-jnp.arange(0, ROPE, 2, dtype=F32) / ROPE)
    ang = jnp.arange(length, dtype=F32)[:, None] * inv[None, :]
    cos, sin = jnp.cos(ang), jnp.sin(ang)
    pad = jnp.zeros((length, LANE // 2 - half), F32)
    cos_t = jnp.concatenate([cos, pad, cos, pad], axis=1)
    sin_t = jnp.concatenate([-sin, pad, sin, pad], axis=1)
    return cos_t, sin_t


def _spread_rope_cols(w):
    out = jnp.zeros((w.shape[0], LANE), w.dtype)
    return out.at[:, _rope_lane_map()].set(w)


def kernel(x, meta_tokens, g_mix_pre, w_in, g_cq, g_ckv, w_uq, w_ukv, lb_logits, g_hgrn, w_o, g_mix_post,
           g_ffn_pre, w_gate, w_up, w_down, g_ffn_post):
    batch, seq, d = x.shape
    depth = w_in.shape[0]
    assert d == D_MODEL and depth == 1 and seq % (2 * CHUNK) == 0
    l = 0
    row2 = lambda g: g.reshape(1, -1).astype(F32)

    in_sizes = [Q_LORA, KV_LORA, ROPE] + [D_MODEL] * N_BIG
    off = np.concatenate([[0], np.cumsum(in_sizes)])
    w = w_in[l]
    wbig = w[:, off[3]:].astype(BF16)
    wlat = jnp.concatenate([w[:, off[0]:off[2]], _spread_rope_cols(w[:, off[2]:off[3]])], axis=1).astype(BF16)
    qk = NOPE + ROPE
    wq = w_uq[l].reshape(Q_LORA, HEADS, qk)
    wq_rope = jnp.zeros((Q_LORA, HEADS, LANE), F32).at[:, :, _rope_lane_map()].set(wq[:, :, NOPE:])
    wuq = jnp.concatenate([wq[:, :, :NOPE], wq_rope], axis=2).reshape(Q_LORA, HEADS * (NOPE + LANE)).astype(BF16)
    wkv = w_ukv[l].reshape(KV_LORA, HEADS, NOPE + V_DIM)
    wukv = jnp.concatenate([wkv[:, :, :NOPE].reshape(KV_LORA, HEADS * NOPE),
                            wkv[:, :, NOPE:].reshape(KV_LORA, HEADS * V_DIM)], axis=1).astype(BF16)
    lb = jax.nn.softmax(lb_logits.astype(F32), axis=0)[l]
    cos_t, sin_t = _rope_tables(N_META + seq)

    proj = functools.partial(_in_proj, gpre=row2(g_mix_pre[l]), wbig=wbig, wlat=wlat, lb=lb)
    xr = x.reshape(batch * seq, d)
    q, kf, kb, bf, cb, v, sg, ga, gh, lat = proj(xr, tm=256, chunk=CHUNK)
    _, kfm, _, bfm, _, vm, _, _, _, latm = proj(meta_tokens.astype(F32), tm=N_META, chunk=N_META)

    b3 = lambda t: t.reshape(batch, seq, t.shape[-1])
    a = _mla(b3(lat), latm, row2(g_cq[l]), row2(g_ckv[l]), wuq, wukv,
             cos_t[N_META:], sin_t[N_META:], cos_t[:N_META], sin_t[:N_META], tq=256, kv_rows=512)
    r = _hgrn(b3(q), b3(kf), b3(kb), b3(bf), b3(cb), b3(v), b3(sg), kfm, bfm, vm, row2(g_hgrn[l]),
              hg=2, norm_rows=256)
    out = _post(xr, a.reshape(batch * seq, d), r.reshape(batch * seq, d), ga, gh,
                w_o[l].astype(BF16), row2(g_mix_post[l]), row2(g_ffn_pre[l]),
                w_gate[l].astype(BF16), w_up[l].astype(BF16), w_down[l].astype(BF16), row2(g_ffn_post[l]),
                tm=256, ff_chunk=1408)
    return out.reshape(batch, seq, d)
```

```python
import functools

import numpy as np
import jax
import jax.numpy as jnp
from jax import lax
from jax.experimental import pallas as pl
from jax.experimental.pallas import tpu as pltpu

F32 = jnp.float32
BF16 = jnp.bfloat16

N_META = 16
EPS = 1e-6
D_MODEL = 1024
HEADS = 8
Q_LORA = 384
KV_LORA = 256
NOPE = 128
ROPE = 64
V_DIM = 128
ROPE_THETA = 10000.0
HEAD_DIM = 128
CHUNK = 64
SUB = 16
N_SUB = CHUNK // SUB

LANE = 128
SUBLANES = 8
VMEM_LIMIT_BYTES = 56 * 1024 * 1024

LAT_WIDTH = Q_LORA + KV_LORA + LANE
N_BIG = 7
NEG_BIG = -1e30
META_PAD = LANE


def _resident(shape):
    zeros = (0,) * len(shape)
    return pl.BlockSpec(shape, lambda *_: zeros, pipeline_mode=pl.Buffered(1))


def _rms(xf, gain):
    return xf * lax.rsqrt(jnp.mean(xf * xf, axis=-1, keepdims=True) + EPS) * gain


def _sigmoid(z):
    return 1.0 / (1.0 + jnp.exp(-z))


def _dot(a, b):
    return jnp.dot(a, b, preferred_element_type=F32)


def _dot_nt(a, b):
    return lax.dot_general(a, b, (((1,), (1,)), ((), ())), preferred_element_type=F32)


def _dot_tn(a, b):
    return lax.dot_general(a, b, (((0,), (0,)), ((), ())), preferred_element_type=F32)


def _chunk_cumsum(g, chunk, reverse):
    rows, width = g.shape
    x = g.reshape(rows // SUBLANES, SUBLANES, width)
    pos = lax.broadcasted_iota(jnp.int32, x.shape, 1)
    step = 1
    while step < SUBLANES:
        if reverse:
            shifted = pltpu.roll(x, SUBLANES - step, axis=1)
            ok = pos < SUBLANES - step
        else:
            shifted = pltpu.roll(x, step, axis=1)
            ok = pos >= step
        x = x + jnp.where(ok, shifted, 0.0)
        step *= 2
    groups = chunk // SUBLANES
    x = x.reshape(rows // chunk, groups, SUBLANES, width)
    out = [None] * groups
    carry = None
    for j in (range(groups - 1, -1, -1) if reverse else range(groups)):
        cur = x[:, j] if carry is None else x[:, j] + carry
        out[j] = cur
        carry = cur[:, 0:1, :] if reverse else cur[:, SUBLANES - 1:SUBLANES, :]
    return jnp.stack(out, axis=1).reshape(rows, width)


def _in_proj_kernel(x_ref, gpre_ref, wbig_ref, wlat_ref, lb_ref,
                    q_ref, kf_ref, kb_ref, bf_ref, cb_ref, v_ref, sg_ref, ga_ref, gh_ref, lat_ref,
                    *, chunk):
    hn = _rms(x_ref[...], gpre_ref[...]).astype(BF16)

    def seg(j):
        return _dot(hn, wbig_ref[:, j * D_MODEL:(j + 1) * D_MODEL])

    z = seg(0)
    q_ref[...] = (z * _sigmoid(z)).astype(BF16)
    for d, (k_ref, s_ref) in enumerate(((kf_ref, bf_ref), (kb_ref, cb_ref))):
        lb = lb_ref[d:d + 1, :]
        f = lb + (1.0 - lb) * _sigmoid(seg(1 + d))
        k_ref[...] = (1.0 - f).astype(BF16)
        s_ref[...] = _chunk_cumsum(jnp.log(f), chunk, reverse=(d == 1))
    v_ref[...] = seg(3).astype(BF16)
    z = seg(4)
    sg_ref[...] = (z * _sigmoid(z)).astype(BF16)
    ga_ref[...] = _sigmoid(seg(5)).astype(BF16)
    gh_ref[...] = _sigmoid(seg(6)).astype(BF16)
    lat_ref[...] = _dot(hn, wlat_ref[...]).astype(BF16)


def _in_proj(rows, gpre, wbig, wlat, lb, *, tm, chunk):
    n = rows.shape[0]
    tile = lambda w: pl.BlockSpec((tm, w), lambda i: (i, 0))
    wide = lambda dt: jax.ShapeDtypeStruct((n, D_MODEL), dt)
    out_dtypes = (BF16, BF16, BF16, F32, F32, BF16, BF16, BF16, BF16)
    return pl.pallas_call(
        functools.partial(_in_proj_kernel, chunk=chunk),
        grid=(n // tm,),
        in_specs=[tile(D_MODEL), _resident(gpre.shape), _resident(wbig.shape),
                  _resident(wlat.shape), _resident(lb.shape)],
        out_specs=[tile(D_MODEL)] * 9 + [tile(LAT_WIDTH)],
        out_shape=[wide(dt) for dt in out_dtypes] + [jax.ShapeDtypeStruct((n, LAT_WIDTH), BF16)],
        compiler_params=pltpu.CompilerParams(
            dimension_semantics=("parallel",), vmem_limit_bytes=VMEM_LIMIT_BYTES),
        name="in_proj",
    )(rows, gpre, wbig, wlat, lb)


def _rope(xr, cos, sin):
    return xr * cos + pltpu.roll(xr, LANE // 2, axis=1) * sin


def _mla_kernel(lat_ref, latm_ref, gcq_ref, gckv_ref, wuq_ref, wukv_ref,
                cos_ref, sin_ref, cosm_ref, sinm_ref,
                a_ref, k_s, v_s, km_s, vm_s, *, tq, kv_rows):
    qi = pl.program_id(1)
    seq = lat_ref.shape[1]
    scale = (NOPE + ROPE) ** -0.5 * np.log2(np.e)
    kw = NOPE + LANE

    def keys_values(ckv, kr_raw, cos, sin):
        ckvn = _rms(ckv.astype(F32), gckv_ref[...]).astype(BF16)
        kv = _dot(ckvn, wukv_ref[...]).astype(BF16)
        kr = _rope(kr_raw.astype(F32), cos, sin).astype(BF16)
        return kv, kr

    def store_heads(kd_s, vd_s, rows, kv, kr):
        for h in range(HEADS):
            kd_s[h, rows, 0:NOPE] = kv[:, h * NOPE:(h + 1) * NOPE]
            kd_s[h, rows, NOPE:kw] = kr
            vd_s[h, rows, :] = kv[:, HEADS * NOPE + h * V_DIM:HEADS * NOPE + (h + 1) * V_DIM]

    @pl.when(qi == 0)
    def _():
        store_heads(km_s, vm_s, slice(None), *keys_values(
            latm_ref[:, Q_LORA:Q_LORA + KV_LORA], latm_ref[:, Q_LORA + KV_LORA:], cosm_ref[...], sinm_ref[...]))

        def body(r, carry):
            rows = pl.ds(pl.multiple_of(r * kv_rows, kv_rows), kv_rows)
            store_heads(k_s, v_s, rows, *keys_values(
                lat_ref[0, rows, Q_LORA:Q_LORA + KV_LORA], lat_ref[0, rows, Q_LORA + KV_LORA:],
                cos_ref[rows, :], sin_ref[rows, :]))
            return carry

        lax.fori_loop(0, seq // kv_rows, body, 0)

    rows = pl.ds(pl.multiple_of(qi * tq, tq), tq)
    cqn = _rms(lat_ref[0, rows, 0:Q_LORA].astype(F32), gcq_ref[...]).astype(BF16)
    qall = _dot(cqn, wuq_ref[...])
    cos = cos_ref[rows, :]
    sin = sin_ref[rows, :]
    meta_valid = lax.broadcasted_iota(jnp.int32, (tq, META_PAD), 1) < N_META
    for h in range(HEADS):
        qn = qall[:, h * kw:h * kw + NOPE] * scale
        qr = _rope(qall[:, h * kw + NOPE:(h + 1) * kw], cos, sin) * scale
        qh = jnp.concatenate([qn, qr], axis=1).astype(BF16)
        s = _dot_nt(qh, k_s[h])
        sm = jnp.where(meta_valid, _dot_nt(qh, km_s[h]), NEG_BIG)
        m = jnp.maximum(jnp.max(s, axis=-1, keepdims=True), jnp.max(sm, axis=-1, keepdims=True))
        p = jnp.exp2(s - m)
        pm = jnp.exp2(sm - m)
        denom = jnp.sum(p, axis=-1, keepdims=True) + jnp.sum(pm, axis=-1, keepdims=True)
        o = _dot(p.astype(BF16), v_s[h]) + _dot(pm.astype(BF16), vm_s[h])
        a_ref[0, :, h * V_DIM:(h + 1) * V_DIM] = (o / denom).astype(BF16)


def _mla(lat, latm, gcq, gckv, wuq, wukv, cos, sin, cosm, sinm, *, tq, kv_rows):
    batch, seq, _ = lat.shape
    kw = NOPE + LANE
    return pl.pallas_call(
        functools.partial(_mla_kernel, tq=tq, kv_rows=kv_rows),
        grid=(batch, seq // tq),
        in_specs=[pl.BlockSpec((1, seq, LAT_WIDTH), lambda b, i: (b, 0, 0))]
        + [_resident(t.shape) for t in (latm, gcq, gckv, wuq, wukv, cos, sin, cosm, sinm)],
        out_specs=pl.BlockSpec((1, tq, HEADS * V_DIM), lambda b, i: (b, i, 0)),
        out_shape=jax.ShapeDtypeStruct((batch, seq, HEADS * V_DIM), BF16),
        scratch_shapes=[pltpu.VMEM((HEADS, seq, kw), BF16), pltpu.VMEM((HEADS, seq, V_DIM), BF16),
                        pltpu.VMEM((HEADS, META_PAD, kw), BF16), pltpu.VMEM((HEADS, META_PAD, V_DIM), BF16)],
        compiler_params=pltpu.CompilerParams(
            dimension_semantics=("parallel", "arbitrary"), vmem_limit_bytes=VMEM_LIMIT_BYTES),
        name="mla",
    )(lat, latm, gcq, gckv, wuq, wukv, cos, sin, cosm, sinm)


def _gla_chunk(q, k, cum, v, st, reverse):
    qf = q.astype(F32)
    kf = k.astype(F32)
    row = lax.broadcasted_iota(jnp.int32, (CHUNK, HEAD_DIM), 0)
    zero = jnp.zeros((1, HEAD_DIM), F32)
    blocks = []
    for i in range(N_SUB):
        lo, hi = i * SUB, (i + 1) * SUB
        if reverse:
            ref = cum[hi:hi + 1, :] if i < N_SUB - 1 else zero
            valid = row >= lo
        else:
            ref = cum[lo - 1:lo, :] if i > 0 else zero
            valid = row < hi
        q_r = qf[lo:hi] * jnp.exp(cum[lo:hi] - ref)
        k_r = jnp.where(valid, kf * jnp.exp(jnp.where(valid, ref - cum, 0.0)), 0.0)
        blocks.append(_dot_nt(q_r.astype(BF16), k_r.astype(BF16)))
    r2 = lax.broadcasted_iota(jnp.int32, (CHUNK, CHUNK), 0)
    c2 = lax.broadcasted_iota(jnp.int32, (CHUNK, CHUNK), 1)
    keep = (r2 <= c2) if reverse else (r2 >= c2)
    a = jnp.where(keep, jnp.concatenate(blocks, axis=0), 0.0)
    total = cum[0:1, :] if reverse else cum[CHUNK - 1:CHUNK, :]
    o = _dot(a.astype(BF16), v) + _dot_nt((qf * jnp.exp(cum)).astype(BF16), st.astype(BF16))
    k_d = (kf * jnp.exp(total - cum)).astype(BF16)
    st_new = st * jnp.exp(total) + _dot_tn(v, k_d)
    return o, st_new


def _hgrn_kernel(q_ref, kf_ref, kb_ref, bf_ref, cb_ref, v_ref, sg_ref,
                 kfm_ref, bfm_ref, vm_ref, gn_ref, r_ref, acc_ref, st_ref, *, hg, norm_rows):
    seq = q_ref.shape[1]
    n_chunks = seq // CHUNK

    for h in range(hg):
        lanes = slice(h * HEAD_DIM, (h + 1) * HEAD_DIM)
        bm = bfm_ref[:, lanes]
        k_d = (kfm_ref[:, lanes].astype(F32) * jnp.exp(bm[N_META - 1:N_META, :] - bm)).astype(BF16)
        st_ref[0, h] = _dot_tn(vm_ref[:, lanes], k_d)
        st_ref[1, h] = jnp.zeros((HEAD_DIM, HEAD_DIM), F32)

    def sweep(c, accumulate):
        rf = pl.ds(pl.multiple_of(c * CHUNK, CHUNK), CHUNK)
        rb = pl.ds(pl.multiple_of((n_chunks - 1 - c) * CHUNK, CHUNK), CHUNK)
        for h in range(hg):
            lanes = slice(h * HEAD_DIM, (h + 1) * HEAD_DIM)
            for d, (rows, k_ref, cum_ref) in enumerate(((rf, kf_ref, bf_ref), (rb, kb_ref, cb_ref))):
                o, st_new = _gla_chunk(q_ref[0, rows, lanes], k_ref[0, rows, lanes], cum_ref[0, rows, lanes],
                                       v_ref[0, rows, lanes], st_ref[d, h], reverse=(d == 1))
                st_ref[d, h] = st_new
                if accumulate:
                    acc_ref[rows, lanes] += o
                else:
                    acc_ref[rows, lanes] = o

    def first(c, carry):
        sweep(c, accumulate=False)
        return carry

    def second(c, carry):
        sweep(c, accumulate=True)
        return carry

    lax.fori_loop(0, n_chunks // 2, first, 0)
    lax.fori_loop(n_chunks // 2, n_chunks, second, 0)

    def finish(i, carry):
        rows = pl.ds(pl.multiple_of(i * norm_rows, norm_rows), norm_rows)
        for h in range(hg):
            lanes = slice(h * HEAD_DIM, (h + 1) * HEAD_DIM)
            o = _rms(acc_ref[rows, lanes], gn_ref[...])
            r_ref[0, rows, lanes] = (o * sg_ref[0, rows, lanes].astype(F32)).astype(BF16)
        return carry

    lax.fori_loop(0, seq // norm_rows, finish, 0)


def _hgrn(q, kf, kb, bf, cb, v, sg, kfm, bfm, vm, gn, *, hg, norm_rows):
    batch, seq, width = q.shape
    gw = hg * HEAD_DIM
    blk = pl.BlockSpec((1, seq, gw), lambda b, g: (b, 0, g))
    meta = pl.BlockSpec((N_META, gw), lambda b, g: (0, g))
    return pl.pallas_call(
        functools.partial(_hgrn_kernel, hg=hg, norm_rows=norm_rows),
        grid=(batch, width // gw),
        in_specs=[blk] * 7 + [meta] * 3 + [_resident(gn.shape)],
        out_specs=blk,
        out_shape=jax.ShapeDtypeStruct((batch, seq, width), BF16),
        scratch_shapes=[pltpu.VMEM((seq, gw), F32), pltpu.VMEM((2, hg, HEAD_DIM, HEAD_DIM), F32)],
        compiler_params=pltpu.CompilerParams(
            dimension_semantics=("parallel", "parallel"), vmem_limit_bytes=VMEM_LIMIT_BYTES),
        name="hgrn",
    )(q, kf, kb, bf, cb, v, sg, kfm, bfm, vm, gn)


def _post_kernel(x_ref, a_ref, r_ref, ga_ref, gh_ref, wo_ref, gpost_ref, gfpre_ref,
                 wg_ref, wu_ref, wd_ref, gfpost_ref, out_ref):
    mixed = (ga_ref[...].astype(F32) * a_ref[...].astype(F32)
             + gh_ref[...].astype(F32) * r_ref[...].astype(F32)).astype(BF16)
    h1 = x_ref[...] + _rms(_dot(mixed, wo_ref[...]), gpost_ref[...])
    hn = _rms(h1, gfpre_ref[...]).astype(BF16)
    gate = _dot(hn, wg_ref[...])
    act = (gate * _sigmoid(gate) * _dot(hn, wu_ref[...])).astype(BF16)
    out_ref[...] = h1 + _rms(_dot(act, wd_ref[...]), gfpost_ref[...])


def _post(x, a, r, ga, gh, wo, gpost, gfpre, wg, wu, wd, gfpost, *, tm):
    n = x.shape[0]
    tile = pl.BlockSpec((tm, D_MODEL), lambda i: (i, 0))
    return pl.pallas_call(
        _post_kernel,
        grid=(n // tm,),
        in_specs=[tile] * 5 + [_resident(w.shape) for w in (wo, gpost, gfpre, wg, wu, wd, gfpost)],
        out_specs=tile,
        out_shape=jax.ShapeDtypeStruct((n, D_MODEL), F32),
        compiler_params=pltpu.CompilerParams(
            dimension_semantics=("parallel",), vmem_limit_bytes=VMEM_LIMIT_BYTES),
        name="post",
    )(x, a, r, ga, gh, wo, gpost, gfpre, wg, wu, wd, gfpost)


def _rope_lane_map():
    half = ROPE // 2
    return np.concatenate([np.arange(half), LANE // 2 + np.arange(half)])


def _rope_tables(length):
    half = ROPE // 2
    inv = ROPE_THETA ** (-jnp.arange(0, ROPE, 2, dtype=F32) / ROPE)
    ang = jnp.arange(length, dtype=F32)[:, None] * inv[None, :]
    cos, sin = jnp.cos(ang), jnp.sin(ang)
    pad = jnp.zeros((length, LANE // 2 - half), F32)
    cos_t = jnp.concatenate([cos, pad, cos, pad], axis=1)
    sin_t = jnp.concatenate([-sin, pad, sin, pad], axis=1)
    return cos_t, sin_t


def _spread_rope_cols(w):
    out = jnp.zeros((w.shape[0], LANE), w.dtype)
    return out.at[:, _rope_lane_map()].set(w)


def kernel(x, meta_tokens, g_mix_pre, w_in, g_cq, g_ckv, w_uq, w_ukv, lb_logits, g_hgrn, w_o, g_mix_post,
           g_ffn_pre, w_gate, w_up, w_down, g_ffn_post):
    batch, seq, d = x.shape
    depth = w_in.shape[0]
    assert d == D_MODEL and depth == 1 and seq % (2 * CHUNK) == 0
    l = 0
    row2 = lambda g: g.reshape(1, -1).astype(F32)

    in_sizes = [Q_LORA, KV_LORA, ROPE] + [D_MODEL] * N_BIG
    off = np.concatenate([[0], np.cumsum(in_sizes)])
    w = w_in[l]
    wbig = w[:, off[3]:].astype(BF16)
    wlat = jnp.concatenate([w[:, off[0]:off[2]], _spread_rope_cols(w[:, off[2]:off[3]])], axis=1).astype(BF16)
    qk = NOPE + ROPE
    wq = w_uq[l].reshape(Q_LORA, HEADS, qk)
    wq_rope = jnp.zeros((Q_LORA, HEADS, LANE), F32).at[:, :, _rope_lane_map()].set(wq[:, :, NOPE:])
    wuq = jnp.concatenate([wq[:, :, :NOPE], wq_rope], axis=2).reshape(Q_LORA, HEADS * (NOPE + LANE)).astype(BF16)
    wkv = w_ukv[l].reshape(KV_LORA, HEADS, NOPE + V_DIM)
    wukv = jnp.concatenate([wkv[:, :, :NOPE].reshape(KV_LORA, HEADS * NOPE),
                            wkv[:, :, NOPE:].reshape(KV_LORA, HEADS * V_DIM)], axis=1).astype(BF16)
    lb = jax.nn.softmax(lb_logits.astype(F32), axis=0)[l]
    cos_t, sin_t = _rope_tables(N_META + seq)

    proj = functools.partial(_in_proj, gpre=row2(g_mix_pre[l]), wbig=wbig, wlat=wlat, lb=lb)
    xr = x.reshape(batch * seq, d)
    q, kf, kb, bf, cb, v, sg, ga, gh, lat = proj(xr, tm=256, chunk=CHUNK)
    _, kfm, _, bfm, _, vm, _, _, _, latm = proj(meta_tokens.astype(F32), tm=N_META, chunk=N_META)

    b3 = lambda t: t.reshape(batch, seq, t.shape[-1])
    pad_meta = lambda t: jnp.pad(t, ((0, META_PAD - N_META), (0, 0)))
    a = _mla(b3(lat), pad_meta(latm), row2(g_cq[l]), row2(g_ckv[l]), wuq, wukv,
             cos_t[N_META:], sin_t[N_META:], pad_meta(cos_t[:N_META]), pad_meta(sin_t[:N_META]),
             tq=512, kv_rows=min(512, seq))
    r = _hgrn(b3(q), b3(kf), b3(kb), b3(bf), b3(cb), b3(v), b3(sg), kfm, bfm, vm, row2(g_hgrn[l]),
              hg=2, norm_rows=256)
    out = _post(xr, a.reshape(batch * seq, d), r.reshape(batch * seq, d), ga, gh,
                w_o[l].astype(BF16), row2(g_mix_post[l]), row2(g_ffn_pre[l]),
                w_gate[l].astype(BF16), w_up[l].astype(BF16), w_down[l].astype(BF16), row2(g_ffn_post[l]),
                tm=512)
    return out.reshape(batch, seq, d)
```

```python
import functools

import numpy as np
import jax
import jax.numpy as jnp
from jax import lax
from jax.experimental import pallas as pl
from jax.experimental.pallas import tpu as pltpu

F32 = jnp.float32
BF16 = jnp.bfloat16

N_META = 16
EPS = 1e-6
D_MODEL = 1024
HEADS = 8
Q_LORA = 384
KV_LORA = 256
NOPE = 128
ROPE = 64
V_DIM = 128
ROPE_THETA = 10000.0
HEAD_DIM = 128
CHUNK = 64
SUB = 16
N_SUB = CHUNK // SUB
PIPE_SLOTS = 2

LANE = 128
SUBLANES = 8
VMEM_LIMIT_BYTES = 56 * 1024 * 1024

LAT_WIDTH = Q_LORA + KV_LORA + LANE
N_BIG = 7
NEG_BIG = -1e30
META_PAD = LANE


def _resident(shape):
    zeros = (0,) * len(shape)
    return pl.BlockSpec(shape, lambda *_: zeros, pipeline_mode=pl.Buffered(1))


def _rms(xf, gain):
    return xf * lax.rsqrt(jnp.mean(xf * xf, axis=-1, keepdims=True) + EPS) * gain


def _sigmoid(z):
    return 1.0 / (1.0 + jnp.exp(-z))


def _dot(a, b):
    return jnp.dot(a, b, preferred_element_type=F32)


def _dot_nt(a, b):
    return lax.dot_general(a, b, (((1,), (1,)), ((), ())), preferred_element_type=F32)


def _dot_tn(a, b):
    return lax.dot_general(a, b, (((0,), (0,)), ((), ())), preferred_element_type=F32)


def _chunk_cumsum(g, chunk, reverse):
    rows, width = g.shape
    x = g.reshape(rows // SUBLANES, SUBLANES, width)
    pos = lax.broadcasted_iota(jnp.int32, x.shape, 1)
    step = 1
    while step < SUBLANES:
        if reverse:
            shifted = pltpu.roll(x, SUBLANES - step, axis=1)
            ok = pos < SUBLANES - step
        else:
            shifted = pltpu.roll(x, step, axis=1)
            ok = pos >= step
        x = x + jnp.where(ok, shifted, 0.0)
        step *= 2
    groups = chunk // SUBLANES
    x = x.reshape(rows // chunk, groups, SUBLANES, width)
    out = [None] * groups
    carry = None
    for j in (range(groups - 1, -1, -1) if reverse else range(groups)):
        cur = x[:, j] if carry is None else x[:, j] + carry
        out[j] = cur
        carry = cur[:, 0:1, :] if reverse else cur[:, SUBLANES - 1:SUBLANES, :]
    return jnp.stack(out, axis=1).reshape(rows, width)


def _in_proj_kernel(x_ref, gpre_ref, wbig_ref, wlat_ref, lb_ref,
                    q_ref, kf_ref, kb_ref, bf_ref, cb_ref, v_ref, sg_ref, ga_ref, gh_ref, lat_ref,
                    *, chunk):
    hn = _rms(x_ref[...], gpre_ref[...]).astype(BF16)

    def seg(j):
        return _dot(hn, wbig_ref[:, j * D_MODEL:(j + 1) * D_MODEL])

    z = seg(0)
    q_ref[...] = (z * _sigmoid(z)).astype(BF16)
    for d, (k_ref, s_ref) in enumerate(((kf_ref, bf_ref), (kb_ref, cb_ref))):
        lb = lb_ref[d:d + 1, :]
        f = lb + (1.0 - lb) * _sigmoid(seg(1 + d))
        k_ref[...] = (1.0 - f).astype(BF16)
        s_ref[...] = _chunk_cumsum(jnp.log(f), chunk, reverse=(d == 1))
    v_ref[...] = seg(3).astype(BF16)
    z = seg(4)
    sg_ref[...] = (z * _sigmoid(z)).astype(BF16)
    ga_ref[...] = _sigmoid(seg(5)).astype(BF16)
    gh_ref[...] = _sigmoid(seg(6)).astype(BF16)
    lat_ref[...] = _dot(hn, wlat_ref[...]).astype(BF16)


def _in_proj(rows, gpre, wbig, wlat, lb, *, tm, chunk):
    n = rows.shape[0]
    tile = lambda w: pl.BlockSpec((tm, w), lambda i: (i, 0))
    wide = lambda dt: jax.ShapeDtypeStruct((n, D_MODEL), dt)
    out_dtypes = (BF16, BF16, BF16, F32, F32, BF16, BF16, BF16, BF16)
    return pl.pallas_call(
        functools.partial(_in_proj_kernel, chunk=chunk),
        grid=(n // tm,),
        in_specs=[tile(D_MODEL), _resident(gpre.shape), _resident(wbig.shape),
                  _resident(wlat.shape), _resident(lb.shape)],
        out_specs=[tile(D_MODEL)] * 9 + [tile(LAT_WIDTH)],
        out_shape=[wide(dt) for dt in out_dtypes] + [jax.ShapeDtypeStruct((n, LAT_WIDTH), BF16)],
        compiler_params=pltpu.CompilerParams(
            dimension_semantics=("parallel",), vmem_limit_bytes=VMEM_LIMIT_BYTES),
        name="in_proj",
    )(rows, gpre, wbig, wlat, lb)


def _rope(xr, cos, sin):
    return xr * cos + pltpu.roll(xr, LANE // 2, axis=1) * sin


def _mla_kernel(lat_ref, latm_ref, gcq_ref, gckv_ref, wuq_ref, wukv_ref,
                cos_ref, sin_ref, cosm_ref, sinm_ref,
                a_ref, k_s, v_s, km_s, vm_s, *, tq, kv_rows):
    qi = pl.program_id(1)
    seq = lat_ref.shape[1]
    scale = (NOPE + ROPE) ** -0.5 * np.log2(np.e)
    kw = NOPE + LANE

    def keys_values(ckv, kr_raw, cos, sin):
        ckvn = _rms(ckv.astype(F32), gckv_ref[...]).astype(BF16)
        kv = _dot(ckvn, wukv_ref[...]).astype(BF16)
        kr = _rope(kr_raw.astype(F32), cos, sin).astype(BF16)
        return kv, kr

    def store_heads(kd_s, vd_s, rows, kv, kr):
        for h in range(HEADS):
            kd_s[h, rows, 0:NOPE] = kv[:, h * NOPE:(h + 1) * NOPE]
            kd_s[h, rows, NOPE:kw] = kr
            vd_s[h, rows, :] = kv[:, HEADS * NOPE + h * V_DIM:HEADS * NOPE + (h + 1) * V_DIM]

    @pl.when(qi == 0)
    def _():
        store_heads(km_s, vm_s, slice(None), *keys_values(
            latm_ref[:, Q_LORA:Q_LORA + KV_LORA], latm_ref[:, Q_LORA + KV_LORA:], cosm_ref[...], sinm_ref[...]))

        def body(r, carry):
            rows = pl.ds(pl.multiple_of(r * kv_rows, kv_rows), kv_rows)
            store_heads(k_s, v_s, rows, *keys_values(
                lat_ref[0, rows, Q_LORA:Q_LORA + KV_LORA], lat_ref[0, rows, Q_LORA + KV_LORA:],
                cos_ref[rows, :], sin_ref[rows, :]))
            return carry

        lax.fori_loop(0, seq // kv_rows, body, 0)

    rows = pl.ds(pl.multiple_of(qi * tq, tq), tq)
    cqn = _rms(lat_ref[0, rows, 0:Q_LORA].astype(F32), gcq_ref[...]).astype(BF16)
    qall = _dot(cqn, wuq_ref[...])
    cos = cos_ref[rows, :]
    sin = sin_ref[rows, :]
    meta_valid = lax.broadcasted_iota(jnp.int32, (tq, META_PAD), 1) < N_META
    for h in range(HEADS):
        qn = qall[:, h * kw:h * kw + NOPE] * scale
        qr = _rope(qall[:, h * kw + NOPE:(h + 1) * kw], cos, sin) * scale
        qh = jnp.concatenate([qn, qr], axis=1).astype(BF16)
        s = _dot_nt(qh, k_s[h])
        sm = jnp.where(meta_valid, _dot_nt(qh, km_s[h]), NEG_BIG)
        m = jnp.maximum(jnp.max(s, axis=-1, keepdims=True), jnp.max(sm, axis=-1, keepdims=True))
        p = jnp.exp2(s - m)
        pm = jnp.exp2(sm - m)
        denom = jnp.sum(p, axis=-1, keepdims=True) + jnp.sum(pm, axis=-1, keepdims=True)
        o = _dot(p.astype(BF16), v_s[h]) + _dot(pm.astype(BF16), vm_s[h])
        a_ref[0, :, h * V_DIM:(h + 1) * V_DIM] = (o / denom).astype(BF16)


def _mla(lat, latm, gcq, gckv, wuq, wukv, cos, sin, cosm, sinm, *, tq, kv_rows):
    batch, seq, _ = lat.shape
    kw = NOPE + LANE
    return pl.pallas_call(
        functools.partial(_mla_kernel, tq=tq, kv_rows=kv_rows),
        grid=(batch, seq // tq),
        in_specs=[pl.BlockSpec((1, seq, LAT_WIDTH), lambda b, i: (b, 0, 0))]
        + [_resident(t.shape) for t in (latm, gcq, gckv, wuq, wukv, cos, sin, cosm, sinm)],
        out_specs=pl.BlockSpec((1, tq, HEADS * V_DIM), lambda b, i: (b, i, 0)),
        out_shape=jax.ShapeDtypeStruct((batch, seq, HEADS * V_DIM), BF16),
        scratch_shapes=[pltpu.VMEM((HEADS, seq, kw), BF16), pltpu.VMEM((HEADS, seq, V_DIM), BF16),
                        pltpu.VMEM((HEADS, META_PAD, kw), BF16), pltpu.VMEM((HEADS, META_PAD, V_DIM), BF16)],
        compiler_params=pltpu.CompilerParams(
            dimension_semantics=("parallel", "arbitrary"), vmem_limit_bytes=VMEM_LIMIT_BYTES),
        name="mla",
    )(lat, latm, gcq, gckv, wuq, wukv, cos, sin, cosm, sinm)


def _gla_prepare(q, k, cum, reverse):
    qf = q.astype(F32)
    kf = k.astype(F32)
    row = lax.broadcasted_iota(jnp.int32, (CHUNK, HEAD_DIM), 0)
    zero = jnp.zeros((1, HEAD_DIM), F32)
    blocks = []
    for i in range(N_SUB):
        lo, hi = i * SUB, (i + 1) * SUB
        if reverse:
            ref = cum[hi:hi + 1, :] if i < N_SUB - 1 else zero
            valid = row >= lo
        else:
            ref = cum[lo - 1:lo, :] if i > 0 else zero
            valid = row < hi
        q_r = qf[lo:hi] * jnp.exp(cum[lo:hi] - ref)
        k_r = jnp.where(valid, kf * jnp.exp(jnp.where(valid, ref - cum, 0.0)), 0.0)
        blocks.append(_dot_nt(q_r.astype(BF16), k_r.astype(BF16)))
    r2 = lax.broadcasted_iota(jnp.int32, (CHUNK, CHUNK), 0)
    c2 = lax.broadcasted_iota(jnp.int32, (CHUNK, CHUNK), 1)
    keep = (r2 <= c2) if reverse else (r2 >= c2)
    a = jnp.where(keep, jnp.concatenate(blocks, axis=0), 0.0).astype(BF16)
    total = cum[0:1, :] if reverse else cum[CHUNK - 1:CHUNK, :]
    q_b = (qf * jnp.exp(cum)).astype(BF16)
    k_d = (kf * jnp.exp(total - cum)).astype(BF16)
    return a, q_b, k_d, jnp.exp(total)


def _hgrn_kernel(q_ref, kf_ref, kb_ref, bf_ref, cb_ref, v_ref, sg_ref,
                 kfm_ref, bfm_ref, vm_ref, gn_ref, r_ref,
                 acc_ref, st_ref, a_s, qb_s, kd_s, dec_s, *, hg, norm_rows):
    seq = q_ref.shape[1]
    n_chunks = seq // CHUNK
    streams = [(h, d) for h in range(hg) for d in range(2)]

    def rows_of(c, d):
        chunk = c if d == 0 else n_chunks - 1 - c
        return pl.ds(pl.multiple_of(chunk * CHUNK, CHUNK), CHUNK)

    def lanes_of(h):
        return slice(h * HEAD_DIM, (h + 1) * HEAD_DIM)

    for h in range(hg):
        bm = bfm_ref[:, lanes_of(h)]
        k_d = (kfm_ref[:, lanes_of(h)].astype(F32) * jnp.exp(bm[N_META - 1:N_META, :] - bm)).astype(BF16)
        st_ref[0, h] = _dot_tn(vm_ref[:, lanes_of(h)], k_d)
        st_ref[1, h] = jnp.zeros((HEAD_DIM, HEAD_DIM), F32)
    acc_ref[...] = jnp.zeros(acc_ref.shape, F32)

    def prepare(c):
        for i, (h, d) in enumerate(streams):
            rows, lanes = rows_of(c, d), lanes_of(h)
            k_ref, cum_ref = (kf_ref, bf_ref) if d == 0 else (kb_ref, cb_ref)
            a, q_b, k_d, dec = _gla_prepare(q_ref[0, rows, lanes], k_ref[0, rows, lanes],
                                            cum_ref[0, rows, lanes], reverse=(d == 1))
            slot = c % PIPE_SLOTS
            a_s[slot, i] = a
            qb_s[slot, i] = q_b
            kd_s[slot, i] = k_d
            dec_s[slot, i] = jnp.broadcast_to(dec, (SUBLANES, HEAD_DIM))

    def apply(c):
        for i, (h, d) in enumerate(streams):
            rows, lanes = rows_of(c, d), lanes_of(h)
            v = v_ref[0, rows, lanes]
            st = st_ref[d, h]
            slot = c % PIPE_SLOTS
            acc_ref[rows, lanes] += _dot(a_s[slot, i], v) + _dot_nt(qb_s[slot, i], st.astype(BF16))
            st_ref[d, h] = st * dec_s[slot, i, 0:1, :] + _dot_tn(v, kd_s[slot, i])

    prepare(0)

    def step(c, carry):
        apply(c)
        prepare(c + 1)
        return carry

    lax.fori_loop(0, n_chunks - 1, step, 0)
    apply(n_chunks - 1)

    def finish(i, carry):
        rows = pl.ds(pl.multiple_of(i * norm_rows, norm_rows), norm_rows)
        for h in range(hg):
            o = _rms(acc_ref[rows, lanes_of(h)], gn_ref[...])
            r_ref[0, rows, lanes_of(h)] = (o * sg_ref[0, rows, lanes_of(h)].astype(F32)).astype(BF16)
        return carry

    lax.fori_loop(0, seq // norm_rows, finish, 0)


def _hgrn(q, kf, kb, bf, cb, v, sg, kfm, bfm, vm, gn, *, hg, norm_rows):
    batch, seq, width = q.shape
    gw = hg * HEAD_DIM
    blk = pl.BlockSpec((1, seq, gw), lambda b, g: (b, 0, g))
    meta = pl.BlockSpec((N_META, gw), lambda b, g: (0, g))
    return pl.pallas_call(
        functools.partial(_hgrn_kernel, hg=hg, norm_rows=norm_rows),
        grid=(batch, width // gw),
        in_specs=[blk] * 7 + [meta] * 3 + [_resident(gn.shape)],
        out_specs=blk,
        out_shape=jax.ShapeDtypeStruct((batch, seq, width), BF16),
        scratch_shapes=[pltpu.VMEM((seq, gw), F32), pltpu.VMEM((2, hg, HEAD_DIM, HEAD_DIM), F32),
                        pltpu.VMEM((PIPE_SLOTS, 2 * hg, CHUNK, CHUNK), BF16),
                        pltpu.VMEM((PIPE_SLOTS, 2 * hg, CHUNK, HEAD_DIM), BF16),
                        pltpu.VMEM((PIPE_SLOTS, 2 * hg, CHUNK, HEAD_DIM), BF16),
                        pltpu.VMEM((PIPE_SLOTS, 2 * hg, SUBLANES, HEAD_DIM), F32)],
        compiler_params=pltpu.CompilerParams(
            dimension_semantics=("parallel", "parallel"), vmem_limit_bytes=VMEM_LIMIT_BYTES),
        name="hgrn",
    )(q, kf, kb, bf, cb, v, sg, kfm, bfm, vm, gn)


def _post_kernel(x_ref, a_ref, r_ref, ga_ref, gh_ref, wo_ref, gpost_ref, gfpre_ref,
                 wg_ref, wu_ref, wd_ref, gfpost_ref, out_ref):
    mixed = (ga_ref[...].astype(F32) * a_ref[...].astype(F32)
             + gh_ref[...].astype(F32) * r_ref[...].astype(F32)).astype(BF16)
    h1 = x_ref[...] + _rms(_dot(mixed, wo_ref[...]), gpost_ref[...])
    hn = _rms(h1, gfpre_ref[...]).astype(BF16)
    gate = _dot(hn, wg_ref[...])
    act = (gate * _sigmoid(gate) * _dot(hn, wu_ref[...])).astype(BF16)
    out_ref[...] = h1 + _rms(_dot(act, wd_ref[...]), gfpost_ref[...])


def _post(x, a, r, ga, gh, wo, gpost, gfpre, wg, wu, wd, gfpost, *, tm):
    n = x.shape[0]
    tile = pl.BlockSpec((tm, D_MODEL), lambda i: (i, 0))
    return pl.pallas_call(
        _post_kernel,
        grid=(n // tm,),
        in_specs=[tile] * 5 + [_resident(w.shape) for w in (wo, gpost, gfpre, wg, wu, wd, gfpost)],
        out_specs=tile,
        out_shape=jax.ShapeDtypeStruct((n, D_MODEL), F32),
        compiler_params=pltpu.CompilerParams(
            dimension_semantics=("parallel",), vmem_limit_bytes=VMEM_LIMIT_BYTES),
        name="post",
    )(x, a, r, ga, gh, wo, gpost, gfpre, wg, wu, wd, gfpost)


def _rope_lane_map():
    half = ROPE // 2
    return np.concatenate([np.arange(half), LANE // 2 + np.arange(half)])


def _rope_tables(length):
    half = ROPE // 2
    inv = ROPE_THETA ** (-jnp.arange(0, ROPE, 2, dtype=F32) / ROPE)
    ang = jnp.arange(length, dtype=F32)[:, None] * inv[None, :]
    cos, sin = jnp.cos(ang), jnp.sin(ang)
    pad = jnp.zeros((length, LANE // 2 - half), F32)
    cos_t = jnp.concatenate([cos, pad, cos, pad], axis=1)
    sin_t = jnp.concatenate([-sin, pad, sin, pad], axis=1)
    return cos_t, sin_t


def _spread_rope_cols(w):
    out = jnp.zeros((w.shape[0], LANE), w.dtype)
    return out.at[:, _rope_lane_map()].set(w)


def kernel(x, meta_tokens, g_mix_pre, w_in, g_cq, g_ckv, w_uq, w_ukv, lb_logits, g_hgrn, w_o, g_mix_post,
           g_ffn_pre, w_gate, w_up, w_down, g_ffn_post):
    batch, seq, d = x.shape
    depth = w_in.shape[0]
    assert d == D_MODEL and depth == 1 and seq % (2 * CHUNK) == 0
    l = 0
    row2 = lambda g: g.reshape(1, -1).astype(F32)

    in_sizes = [Q_LORA, KV_LORA, ROPE] + [D_MODEL] * N_BIG
    off = np.concatenate([[0], np.cumsum(in_sizes)])
    w = w_in[l]
    wbig = w[:, off[3]:].astype(BF16)
    wlat = jnp.concatenate([w[:, off[0]:off[2]], _spread_rope_cols(w[:, off[2]:off[3]])], axis=1).astype(BF16)
    qk = NOPE + ROPE
    wq = w_uq[l].reshape(Q_LORA, HEADS, qk)
    wq_rope = jnp.zeros((Q_LORA, HEADS, LANE), F32).at[:, :, _rope_lane_map()].set(wq[:, :, NOPE:])
    wuq = jnp.concatenate([wq[:, :, :NOPE], wq_rope], axis=2).reshape(Q_LORA, HEADS * (NOPE + LANE)).astype(BF16)
    wkv = w_ukv[l].reshape(KV_LORA, HEADS, NOPE + V_DIM)
    wukv = jnp.concatenate([wkv[:, :, :NOPE].reshape(KV_LORA, HEADS * NOPE),
                            wkv[:, :, NOPE:].reshape(KV_LORA, HEADS * V_DIM)], axis=1).astype(BF16)
    lb = jax.nn.softmax(lb_logits.astype(F32), axis=0)[l]
    cos_t, sin_t = _rope_tables(N_META + seq)

    proj = functools.partial(_in_proj, gpre=row2(g_mix_pre[l]), wbig=wbig, wlat=wlat, lb=lb)
    xr = x.reshape(batch * seq, d)
    q, kf, kb, bf, cb, v, sg, ga, gh, lat = proj(xr, tm=256, chunk=CHUNK)
    _, kfm, _, bfm, _, vm, _, _, _, latm = proj(meta_tokens.astype(F32), tm=N_META, chunk=N_META)

    b3 = lambda t: t.reshape(batch, seq, t.shape[-1])
    pad_meta = lambda t: jnp.pad(t, ((0, META_PAD - N_META), (0, 0)))
    a = _mla(b3(lat), pad_meta(latm), row2(g_cq[l]), row2(g_ckv[l]), wuq, wukv,
             cos_t[N_META:], sin_t[N_META:], pad_meta(cos_t[:N_META]), pad_meta(sin_t[:N_META]),
             tq=512, kv_rows=min(512, seq))
    r = _hgrn(b3(q), b3(kf), b3(kb), b3(bf), b3(cb), b3(v), b3(sg), kfm, bfm, vm, row2(g_hgrn[l]),
              hg=4, norm_rows=256)
    out = _post(xr, a.reshape(batch * seq, d), r.reshape(batch * seq, d), ga, gh,
                w_o[l].astype(BF16), row2(g_mix_post[l]), row2(g_ffn_pre[l]),
                w_gate[l].astype(BF16), w_up[l].astype(BF16), w_down[l].astype(BF16), row2(g_ffn_post[l]),
                tm=512)
    return out.reshape(batch, seq, d)
```

```python
import functools

import numpy as np
import jax
import jax.numpy as jnp
from jax import lax
from jax.experimental import pallas as pl
from jax.experimental.pallas import tpu as pltpu

F32 = jnp.float32
BF16 = jnp.bfloat16

N_META = 16
EPS = 1e-6
D_MODEL = 1024
HEADS = 8
Q_LORA = 384
KV_LORA = 256
NOPE = 128
ROPE = 64
V_DIM = 128
ROPE_THETA = 10000.0
HEAD_DIM = 128
CHUNK = 64
SUB = 16
N_SUB = CHUNK // SUB
PIPE_SLOTS = 2

LANE = 128
SUBLANES = 8
VMEM_LIMIT_BYTES = 56 * 1024 * 1024

LAT_WIDTH = Q_LORA + KV_LORA + LANE
N_BIG = 7
NEG_BIG = -1e30
META_PAD = LANE


def _resident(shape):
    zeros = (0,) * len(shape)
    return pl.BlockSpec(shape, lambda *_: zeros, pipeline_mode=pl.Buffered(1))


def _rms(xf, gain):
    return xf * lax.rsqrt(jnp.mean(xf * xf, axis=-1, keepdims=True) + EPS) * gain


def _sigmoid(z):
    return 1.0 / (1.0 + jnp.exp(-z))


def _dot(a, b):
    return jnp.dot(a, b, preferred_element_type=F32)


def _dot_nt(a, b):
    return lax.dot_general(a, b, (((1,), (1,)), ((), ())), preferred_element_type=F32)


def _dot_tn(a, b):
    return lax.dot_general(a, b, (((0,), (0,)), ((), ())), preferred_element_type=F32)


def _chunk_cumsum_into(out_ref, g, chunk, reverse):
    rows, width = g.shape
    x = g.reshape(rows // SUBLANES, SUBLANES, width)
    pos = lax.broadcasted_iota(jnp.int32, x.shape, 1)
    step = 1
    while step < SUBLANES:
        if reverse:
            shifted = pltpu.roll(x, SUBLANES - step, axis=1)
            ok = pos < SUBLANES - step
        else:
            shifted = pltpu.roll(x, step, axis=1)
            ok = pos >= step
        x = x + jnp.where(ok, shifted, 0.0)
        step *= 2
    out_ref[...] = x.reshape(rows, width)
    groups = chunk // SUBLANES
    for j in (range(groups - 2, -1, -1) if reverse else range(1, groups)):
        for c in range(rows // chunk):
            lo = c * chunk + j * SUBLANES
            edge = lo + SUBLANES if reverse else lo - 1
            out_ref[lo:lo + SUBLANES, :] += out_ref[edge:edge + 1, :]


def _in_proj_kernel(x_ref, gpre_ref, wbig_ref, wlat_ref, lb_ref,
                    q_ref, kf_ref, kb_ref, bf_ref, cb_ref, v_ref, sg_ref, ga_ref, gh_ref, lat_ref,
                    *, chunk):
    hn = _rms(x_ref[...], gpre_ref[...]).astype(BF16)

    proj = _dot(hn, wbig_ref[...])

    def seg(j):
        return proj[:, j * D_MODEL:(j + 1) * D_MODEL]

    z = seg(0)
    q_ref[...] = (z * _sigmoid(z)).astype(BF16)
    for d, (k_ref, s_ref) in enumerate(((kf_ref, bf_ref), (kb_ref, cb_ref))):
        lb = lb_ref[d:d + 1, :]
        f = lb + (1.0 - lb) * _sigmoid(seg(1 + d))
        k_ref[...] = (1.0 - f).astype(BF16)
        _chunk_cumsum_into(s_ref, jnp.log(f), chunk, reverse=(d == 1))
    v_ref[...] = seg(3).astype(BF16)
    z = seg(4)
    sg_ref[...] = (z * _sigmoid(z)).astype(BF16)
    ga_ref[...] = _sigmoid(seg(5)).astype(BF16)
    gh_ref[...] = _sigmoid(seg(6)).astype(BF16)
    lat_ref[...] = _dot(hn, wlat_ref[...]).astype(BF16)


def _in_proj(rows, gpre, wbig, wlat, lb, *, tm, chunk):
    n = rows.shape[0]
    tile = lambda w: pl.BlockSpec((tm, w), lambda i: (i, 0))
    wide = lambda dt: jax.ShapeDtypeStruct((n, D_MODEL), dt)
    out_dtypes = (BF16, BF16, BF16, F32, F32, BF16, BF16, BF16, BF16)
    return pl.pallas_call(
        functools.partial(_in_proj_kernel, chunk=chunk),
        grid=(n // tm,),
        in_specs=[tile(D_MODEL), _resident(gpre.shape), _resident(wbig.shape),
                  _resident(wlat.shape), _resident(lb.shape)],
        out_specs=[tile(D_MODEL)] * 9 + [tile(LAT_WIDTH)],
        out_shape=[wide(dt) for dt in out_dtypes] + [jax.ShapeDtypeStruct((n, LAT_WIDTH), BF16)],
        compiler_params=pltpu.CompilerParams(
            dimension_semantics=("parallel",), vmem_limit_bytes=VMEM_LIMIT_BYTES),
        name="in_proj",
    )(rows, gpre, wbig, wlat, lb)


def _rope(xr, cos, sin):
    return xr * cos + pltpu.roll(xr, LANE // 2, axis=1) * sin


def _mla_kernel(lat_ref, latm_ref, gcq_ref, gckv_ref, wuq_ref, wukv_ref,
                cos_ref, sin_ref, cosm_ref, sinm_ref,
                a_ref, k_s, v_s, km_s, vm_s, *, tq, kv_rows):
    qi = pl.program_id(1)
    seq = lat_ref.shape[1]
    scale = (NOPE + ROPE) ** -0.5 * np.log2(np.e)
    kw = NOPE + LANE

    def keys_values(ckv, kr_raw, cos, sin):
        ckvn = _rms(ckv.astype(F32), gckv_ref[...]).astype(BF16)
        kv = _dot(ckvn, wukv_ref[...]).astype(BF16)
        kr = _rope(kr_raw.astype(F32), cos, sin).astype(BF16)
        return kv, kr

    def store_heads(kd_s, vd_s, rows, kv, kr):
        for h in range(HEADS):
            kd_s[h, rows, 0:NOPE] = kv[:, h * NOPE:(h + 1) * NOPE]
            kd_s[h, rows, NOPE:kw] = kr
            vd_s[h, rows, 0:V_DIM] = kv[:, HEADS * NOPE + h * V_DIM:HEADS * NOPE + (h + 1) * V_DIM]
            vd_s[h, rows, V_DIM:2 * V_DIM] = jnp.ones((kv.shape[0], V_DIM), BF16)

    @pl.when(qi == 0)
    def _():
        store_heads(km_s, vm_s, slice(None), *keys_values(
            latm_ref[:, Q_LORA:Q_LORA + KV_LORA], latm_ref[:, Q_LORA + KV_LORA:], cosm_ref[...], sinm_ref[...]))

        def body(r, carry):
            rows = pl.ds(pl.multiple_of(r * kv_rows, kv_rows), kv_rows)
            store_heads(k_s, v_s, rows, *keys_values(
                lat_ref[0, rows, Q_LORA:Q_LORA + KV_LORA], lat_ref[0, rows, Q_LORA + KV_LORA:],
                cos_ref[rows, :], sin_ref[rows, :]))
            return carry

        lax.fori_loop(0, seq // kv_rows, body, 0)

    rows = pl.ds(pl.multiple_of(qi * tq, tq), tq)
    cqn = _rms(lat_ref[0, rows, 0:Q_LORA].astype(F32), gcq_ref[...]).astype(BF16)
    qall = _dot(cqn, wuq_ref[...])
    cos = cos_ref[rows, :]
    sin = sin_ref[rows, :]
    meta_valid = lax.broadcasted_iota(jnp.int32, (tq, META_PAD), 1) < N_META
    for h in range(HEADS):
        qn = qall[:, h * kw:h * kw + NOPE] * scale
        qr = _rope(qall[:, h * kw + NOPE:(h + 1) * kw], cos, sin) * scale
        qh = jnp.concatenate([qn, qr], axis=1).astype(BF16)
        s = _dot_nt(qh, k_s[h])
        sm = jnp.where(meta_valid, _dot_nt(qh, km_s[h]), NEG_BIG)
        m = jnp.maximum(jnp.max(s, axis=-1, keepdims=True), jnp.max(sm, axis=-1, keepdims=True))
        p = jnp.exp2(s - m)
        pm = jnp.exp2(sm - m)
        o = _dot(p.astype(BF16), v_s[h]) + _dot(pm.astype(BF16), vm_s[h])
        a_ref[0, :, h * V_DIM:(h + 1) * V_DIM] = (o[:, 0:V_DIM] / o[:, V_DIM:2 * V_DIM]).astype(BF16)


def _mla(lat, latm, gcq, gckv, wuq, wukv, cos, sin, cosm, sinm, *, tq, kv_rows):
    batch, seq, _ = lat.shape
    kw = NOPE + LANE
    return pl.pallas_call(
        functools.partial(_mla_kernel, tq=tq, kv_rows=kv_rows),
        grid=(batch, seq // tq),
        in_specs=[pl.BlockSpec((1, seq, LAT_WIDTH), lambda b, i: (b, 0, 0))]
        + [_resident(t.shape) for t in (latm, gcq, gckv, wuq, wukv, cos, sin, cosm, sinm)],
        out_specs=pl.BlockSpec((1, tq, HEADS * V_DIM), lambda b, i: (b, i, 0)),
        out_shape=jax.ShapeDtypeStruct((batch, seq, HEADS * V_DIM), BF16),
        scratch_shapes=[pltpu.VMEM((HEADS, seq, kw), BF16), pltpu.VMEM((HEADS, seq, 2 * V_DIM), BF16),
                        pltpu.VMEM((HEADS, META_PAD, kw), BF16), pltpu.VMEM((HEADS, META_PAD, 2 * V_DIM), BF16)],
        compiler_params=pltpu.CompilerParams(
            dimension_semantics=("parallel", "arbitrary"), vmem_limit_bytes=VMEM_LIMIT_BYTES),
        name="mla",
    )(lat, latm, gcq, gckv, wuq, wukv, cos, sin, cosm, sinm)


def _gla_prepare(q, k, cum, reverse):
    qf = q.astype(F32)
    kf = k.astype(F32)
    row = lax.broadcasted_iota(jnp.int32, (CHUNK, HEAD_DIM), 0)
    zero = jnp.zeros((1, HEAD_DIM), F32)
    blocks = []
    for i in range(N_SUB):
        lo, hi = i * SUB, (i + 1) * SUB
        if reverse:
            ref = cum[hi:hi + 1, :] if i < N_SUB - 1 else zero
            valid = row >= lo
        else:
            ref = cum[lo - 1:lo, :] if i > 0 else zero
            valid = row < hi
        q_r = qf[lo:hi] * jnp.exp(cum[lo:hi] - ref)
        k_r = jnp.where(valid, kf * jnp.exp(jnp.where(valid, ref - cum, 0.0)), 0.0)
        blocks.append(_dot_nt(q_r.astype(BF16), k_r.astype(BF16)))
    r2 = lax.broadcasted_iota(jnp.int32, (CHUNK, CHUNK), 0)
    c2 = lax.broadcasted_iota(jnp.int32, (CHUNK, CHUNK), 1)
    keep = (r2 <= c2) if reverse else (r2 >= c2)
    a = jnp.where(keep, jnp.concatenate(blocks, axis=0), 0.0).astype(BF16)
    total = cum[0:1, :] if reverse else cum[CHUNK - 1:CHUNK, :]
    q_b = (qf * jnp.exp(cum)).astype(BF16)
    k_d = (kf * jnp.exp(total - cum)).astype(BF16)
    return a, q_b, k_d, jnp.exp(total)


def _hgrn_kernel(q_ref, kf_ref, kb_ref, bf_ref, cb_ref, v_ref, sg_ref,
                 kfm_ref, bfm_ref, vm_ref, gn_ref, r_ref,
                 acc_ref, st_ref, a_s, qb_s, kd_s, dec_s, *, hg):
    seq = q_ref.shape[1]
    n_chunks = seq // CHUNK
    streams = [(h, d) for h in range(hg) for d in range(2)]

    def rows_of(c, d):
        chunk = c if d == 0 else n_chunks - 1 - c
        return pl.ds(pl.multiple_of(chunk * CHUNK, CHUNK), CHUNK)

    def lanes_of(h):
        return slice(h * HEAD_DIM, (h + 1) * HEAD_DIM)

    for h in range(hg):
        bm = bfm_ref[:, lanes_of(h)]
        k_d = (kfm_ref[:, lanes_of(h)].astype(F32) * jnp.exp(bm[N_META - 1:N_META, :] - bm)).astype(BF16)
        st_ref[0, h] = _dot_tn(vm_ref[:, lanes_of(h)], k_d)
        st_ref[1, h] = jnp.zeros((HEAD_DIM, HEAD_DIM), F32)
    acc_ref[...] = jnp.zeros(acc_ref.shape, F32)

    def prepare(c):
        for i, (h, d) in enumerate(streams):
            rows, lanes = rows_of(c, d), lanes_of(h)
            k_ref, cum_ref = (kf_ref, bf_ref) if d == 0 else (kb_ref, cb_ref)
            a, q_b, k_d, dec = _gla_prepare(q_ref[0, rows, lanes], k_ref[0, rows, lanes],
                                            cum_ref[0, rows, lanes], reverse=(d == 1))
            slot = c % PIPE_SLOTS
            a_s[slot, i] = a
            qb_s[slot, i] = q_b
            kd_s[slot, i] = k_d
            dec_s[slot, i] = jnp.broadcast_to(dec, (SUBLANES, HEAD_DIM))

    def apply(c, finalize):
        for i, (h, d) in enumerate(streams):
            rows, lanes = rows_of(c, d), lanes_of(h)
            v = v_ref[0, rows, lanes]
            st = st_ref[d, h]
            slot = c % PIPE_SLOTS
            acc_ref[rows, lanes] += _dot(a_s[slot, i], v) + _dot_nt(qb_s[slot, i], st.astype(BF16))
            st_ref[d, h] = st * dec_s[slot, i, 0:1, :] + _dot_tn(v, kd_s[slot, i])
        if finalize:
            for h in range(hg):
                for d in range(2):
                    rows, lanes = rows_of(c, d), lanes_of(h)
                    o = _rms(acc_ref[rows, lanes], gn_ref[...])
                    r_ref[0, rows, lanes] = (o * sg_ref[0, rows, lanes].astype(F32)).astype(BF16)

    def step(finalize):
        def body(c, carry):
            apply(c, finalize)
            prepare(c + 1)
            return carry
        return body

    prepare(0)
    lax.fori_loop(0, n_chunks // 2, step(False), 0)
    lax.fori_loop(n_chunks // 2, n_chunks - 1, step(True), 0)
    apply(n_chunks - 1, True)


def _hgrn(q, kf, kb, bf, cb, v, sg, kfm, bfm, vm, gn, *, hg):
    batch, seq, width = q.shape
    gw = hg * HEAD_DIM
    blk = pl.BlockSpec((1, seq, gw), lambda b, g: (b, 0, g))
    meta = pl.BlockSpec((N_META, gw), lambda b, g: (0, g))
    return pl.pallas_call(
        functools.partial(_hgrn_kernel, hg=hg),
        grid=(batch, width // gw),
        in_specs=[blk] * 7 + [meta] * 3 + [_resident(gn.shape)],
        out_specs=blk,
        out_shape=jax.ShapeDtypeStruct((batch, seq, width), BF16),
        scratch_shapes=[pltpu.VMEM((seq, gw), F32), pltpu.VMEM((2, hg, HEAD_DIM, HEAD_DIM), F32),
                        pltpu.VMEM((PIPE_SLOTS, 2 * hg, CHUNK, CHUNK), BF16),
                        pltpu.VMEM((PIPE_SLOTS, 2 * hg, CHUNK, HEAD_DIM), BF16),
                        pltpu.VMEM((PIPE_SLOTS, 2 * hg, CHUNK, HEAD_DIM), BF16),
                        pltpu.VMEM((PIPE_SLOTS, 2 * hg, SUBLANES, HEAD_DIM), F32)],
        compiler_params=pltpu.CompilerParams(
            dimension_semantics=("parallel", "parallel"), vmem_limit_bytes=VMEM_LIMIT_BYTES),
        name="hgrn",
    )(q, kf, kb, bf, cb, v, sg, kfm, bfm, vm, gn)


def _post_kernel(x_ref, a_ref, r_ref, ga_ref, gh_ref, wo_ref, gpost_ref, gfpre_ref,
                 wg_ref, wu_ref, wd_ref, gfpost_ref, out_ref):
    mixed = (ga_ref[...].astype(F32) * a_ref[...].astype(F32)
             + gh_ref[...].astype(F32) * r_ref[...].astype(F32)).astype(BF16)
    h1 = x_ref[...] + _rms(_dot(mixed, wo_ref[...]), gpost_ref[...])
    hn = _rms(h1, gfpre_ref[...]).astype(BF16)
    gate = _dot(hn, wg_ref[...])
    act = (gate * _sigmoid(gate) * _dot(hn, wu_ref[...])).astype(BF16)
    out_ref[...] = h1 + _rms(_dot(act, wd_ref[...]), gfpost_ref[...])


def _post(x, a, r, ga, gh, wo, gpost, gfpre, wg, wu, wd, gfpost, *, tm):
    n = x.shape[0]
    tile = pl.BlockSpec((tm, D_MODEL), lambda i: (i, 0))
    return pl.pallas_call(
        _post_kernel,
        grid=(n // tm,),
        in_specs=[tile] * 5 + [_resident(w.shape) for w in (wo, gpost, gfpre, wg, wu, wd, gfpost)],
        out_specs=tile,
        out_shape=jax.ShapeDtypeStruct((n, D_MODEL), F32),
        compiler_params=pltpu.CompilerParams(
            dimension_semantics=("parallel",), vmem_limit_bytes=VMEM_LIMIT_BYTES),
        name="post",
    )(x, a, r, ga, gh, wo, gpost, gfpre, wg, wu, wd, gfpost)


def _rope_lane_map():
    half = ROPE // 2
    return np.concatenate([np.arange(half), LANE // 2 + np.arange(half)])


def _rope_tables(length):
    half = ROPE // 2
    inv = ROPE_THETA ** (-jnp.arange(0, ROPE, 2, dtype=F32) / ROPE)
    ang = jnp.arange(length, dtype=F32)[:, None] * inv[None, :]
    cos, sin = jnp.cos(ang), jnp.sin(ang)
    pad = jnp.zeros((length, LANE // 2 - half), F32)
    cos_t = jnp.concatenate([cos, pad, cos, pad], axis=1)
    sin_t = jnp.concatenate([-sin, pad, sin, pad], axis=1)
    return cos_t, sin_t


def _spread_rope_cols(w):
    out = jnp.zeros((w.shape[0], LANE), w.dtype)
    return out.at[:, _rope_lane_map()].set(w)


def kernel(x, meta_tokens, g_mix_pre, w_in, g_cq, g_ckv, w_uq, w_ukv, lb_logits, g_hgrn, w_o, g_mix_post,
           g_ffn_pre, w_gate, w_up, w_down, g_ffn_post):
    batch, seq, d = x.shape
    depth = w_in.shape[0]
    assert d == D_MODEL and depth == 1 and seq % (2 * CHUNK) == 0
    l = 0
    row2 = lambda g: g.reshape(1, -1).astype(F32)

    in_sizes = [Q_LORA, KV_LORA, ROPE] + [D_MODEL] * N_BIG
    off = np.concatenate([[0], np.cumsum(in_sizes)])
    w = w_in[l]
    wbig = w[:, off[3]:].astype(BF16)
    wlat = jnp.concatenate([w[:, off[0]:off[2]], _spread_rope_cols(w[:, off[2]:off[3]])], axis=1).astype(BF16)
    qk = NOPE + ROPE
    wq = w_uq[l].reshape(Q_LORA, HEADS, qk)
    wq_rope = jnp.zeros((Q_LORA, HEADS, LANE), F32).at[:, :, _rope_lane_map()].set(wq[:, :, NOPE:])
    wuq = jnp.concatenate([wq[:, :, :NOPE], wq_rope], axis=2).reshape(Q_LORA, HEADS * (NOPE + LANE)).astype(BF16)
    wkv = w_ukv[l].reshape(KV_LORA, HEADS, NOPE + V_DIM)
    wukv = jnp.concatenate([wkv[:, :, :NOPE].reshape(KV_LORA, HEADS * NOPE),
                            wkv[:, :, NOPE:].reshape(KV_LORA, HEADS * V_DIM)], axis=1).astype(BF16)
    lb = jax.nn.softmax(lb_logits.astype(F32), axis=0)[l]
    cos_t, sin_t = _rope_tables(N_META + seq)

    proj = functools.partial(_in_proj, gpre=row2(g_mix_pre[l]), wbig=wbig, wlat=wlat, lb=lb)
    xr = x.reshape(batch * seq, d)
    q, kf, kb, bf, cb, v, sg, ga, gh, lat = proj(xr, tm=256, chunk=CHUNK)
    _, kfm, _, bfm, _, vm, _, _, _, latm = proj(meta_tokens.astype(F32), tm=N_META, chunk=N_META)

    b3 = lambda t: t.reshape(batch, seq, t.shape[-1])
    pad_meta = lambda t: jnp.pad(t, ((0, META_PAD - N_META), (0, 0)))
    a = _mla(b3(lat), pad_meta(latm), row2(g_cq[l]), row2(g_ckv[l]), wuq, wukv,
             cos_t[N_META:], sin_t[N_META:], pad_meta(cos_t[:N_META]), pad_meta(sin_t[:N_META]),
             tq=512, kv_rows=min(512, seq))
    r = _hgrn(b3(q), b3(kf), b3(kb), b3(bf), b3(cb), b3(v), b3(sg), kfm, bfm, vm, row2(g_hgrn[l]),
              hg=4)
    out = _post(xr, a.reshape(batch * seq, d), r.reshape(batch * seq, d), ga, gh,
                w_o[l].astype(BF16), row2(g_mix_post[l]), row2(g_ffn_pre[l]),
                w_gate[l].astype(BF16), w_up[l].astype(BF16), w_down[l].astype(BF16), row2(g_ffn_post[l]),
                tm=512)
    return out.reshape(batch, seq, d)
```

```python
import functools

import numpy as np
import jax
import jax.numpy as jnp
from jax import lax
from jax.experimental import pallas as pl
from jax.experimental.pallas import tpu as pltpu

F32 = jnp.float32
BF16 = jnp.bfloat16

N_META = 16
EPS = 1e-6
D_MODEL = 1024
HEADS = 8
Q_LORA = 384
KV_LORA = 256
NOPE = 128
ROPE = 64
V_DIM = 128
ROPE_THETA = 10000.0
HEAD_DIM = 128
CHUNK = 64
SUB = 16
N_SUB = CHUNK // SUB
PIPE_SLOTS = 2

LANE = 128
SUBLANES = 8
VMEM_LIMIT_BYTES = 56 * 1024 * 1024

LAT_WIDTH = Q_LORA + KV_LORA + LANE
N_BIG = 7
NEG_BIG = -1e30
META_PAD = LANE


def _resident(shape):
    zeros = (0,) * len(shape)
    return pl.BlockSpec(shape, lambda *_: zeros, pipeline_mode=pl.Buffered(1))


def _rms(xf, gain):
    return xf * lax.rsqrt(jnp.mean(xf * xf, axis=-1, keepdims=True) + EPS) * gain


def _sigmoid(z):
    return 1.0 / (1.0 + jnp.exp(-z))


def _dot(a, b):
    return jnp.dot(a, b, preferred_element_type=F32)


def _dot_nt(a, b):
    return lax.dot_general(a, b, (((1,), (1,)), ((), ())), preferred_element_type=F32)


def _dot_tn(a, b):
    return lax.dot_general(a, b, (((0,), (0,)), ((), ())), preferred_element_type=F32)


def _chunk_cumsum(g, chunk, reverse):
    rows, width = g.shape
    x = g.reshape(rows // SUBLANES, SUBLANES, width)
    pos = lax.broadcasted_iota(jnp.int32, x.shape, 1)
    step = 1
    while step < SUBLANES:
        if reverse:
            shifted = pltpu.roll(x, SUBLANES - step, axis=1)
            ok = pos < SUBLANES - step
        else:
            shifted = pltpu.roll(x, step, axis=1)
            ok = pos >= step
        x = x + jnp.where(ok, shifted, 0.0)
        step *= 2
    groups = chunk // SUBLANES
    x = x.reshape(rows // chunk, groups, SUBLANES, width)
    out = [None] * groups
    carry = None
    for j in (range(groups - 1, -1, -1) if reverse else range(groups)):
        cur = x[:, j] if carry is None else x[:, j] + carry
        out[j] = cur
        carry = cur[:, 0:1, :] if reverse else cur[:, SUBLANES - 1:SUBLANES, :]
    return jnp.stack(out, axis=1).reshape(rows, width)


def _in_proj_kernel(x_ref, gpre_ref, wbig_ref, wlat_ref, lb_ref,
                    q_ref, kf_ref, kb_ref, bf_ref, cb_ref, v_ref, sg_ref, ga_ref, gh_ref, lat_ref,
                    *, chunk):
    hn = _rms(x_ref[...], gpre_ref[...]).astype(BF16)

    proj = _dot(hn, wbig_ref[...])

    def seg(j):
        return proj[:, j * D_MODEL:(j + 1) * D_MODEL]

    z = seg(0)
    q_ref[...] = (z * _sigmoid(z)).astype(BF16)
    for d, (k_ref, s_ref) in enumerate(((kf_ref, bf_ref), (kb_ref, cb_ref))):
        lb = lb_ref[d:d + 1, :]
        f = lb + (1.0 - lb) * _sigmoid(seg(1 + d))
        k_ref[...] = (1.0 - f).astype(BF16)
        s_ref[...] = _chunk_cumsum(jnp.log(f), chunk, reverse=(d == 1))
    v_ref[...] = seg(3).astype(BF16)
    z = seg(4)
    sg_ref[...] = (z * _sigmoid(z)).astype(BF16)
    ga_ref[...] = _sigmoid(seg(5)).astype(BF16)
    gh_ref[...] = _sigmoid(seg(6)).astype(BF16)
    lat_ref[...] = _dot(hn, wlat_ref[...]).astype(BF16)


def _in_proj(rows, gpre, wbig, wlat, lb, *, tm, chunk):
    n = rows.shape[0]
    tile = lambda w: pl.BlockSpec((tm, w), lambda i: (i, 0))
    wide = lambda dt: jax.ShapeDtypeStruct((n, D_MODEL), dt)
    out_dtypes = (BF16, BF16, BF16, F32, F32, BF16, BF16, BF16, BF16)
    return pl.pallas_call(
        functools.partial(_in_proj_kernel, chunk=chunk),
        grid=(n // tm,),
        in_specs=[tile(D_MODEL), _resident(gpre.shape), _resident(wbig.shape),
                  _resident(wlat.shape), _resident(lb.shape)],
        out_specs=[tile(D_MODEL)] * 9 + [tile(LAT_WIDTH)],
        out_shape=[wide(dt) for dt in out_dtypes] + [jax.ShapeDtypeStruct((n, LAT_WIDTH), BF16)],
        compiler_params=pltpu.CompilerParams(
            dimension_semantics=("parallel",), vmem_limit_bytes=VMEM_LIMIT_BYTES),
        name="in_proj",
    )(rows, gpre, wbig, wlat, lb)


def _rope(xr, cos, sin):
    return xr * cos + pltpu.roll(xr, LANE // 2, axis=1) * sin


def _mla_kernel(lat_ref, latm_ref, gcq_ref, gckv_ref, wuq_ref, wukv_ref,
                cos_ref, sin_ref, cosm_ref, sinm_ref,
                a_ref, k_s, v_s, km_s, vm_s, *, tq, kv_rows):
    qi = pl.program_id(1)
    seq = lat_ref.shape[1]
    scale = (NOPE + ROPE) ** -0.5 * np.log2(np.e)
    kw = NOPE + LANE

    def keys_values(ckv, kr_raw, cos, sin):
        ckvn = _rms(ckv.astype(F32), gckv_ref[...]).astype(BF16)
        kv = _dot(ckvn, wukv_ref[...]).astype(BF16)
        kr = _rope(kr_raw.astype(F32), cos, sin).astype(BF16)
        return kv, kr

    def store_heads(kd_s, vd_s, rows, kv, kr):
        for h in range(HEADS):
            kd_s[h, rows, 0:NOPE] = kv[:, h * NOPE:(h + 1) * NOPE]
            kd_s[h, rows, NOPE:kw] = kr
            vd_s[h, rows, 0:V_DIM] = kv[:, HEADS * NOPE + h * V_DIM:HEADS * NOPE + (h + 1) * V_DIM]
            vd_s[h, rows, V_DIM:2 * V_DIM] = jnp.ones((kv.shape[0], V_DIM), BF16)

    @pl.when(qi == 0)
    def _():
        store_heads(km_s, vm_s, slice(None), *keys_values(
            latm_ref[:, Q_LORA:Q_LORA + KV_LORA], latm_ref[:, Q_LORA + KV_LORA:], cosm_ref[...], sinm_ref[...]))

        def body(r, carry):
            rows = pl.ds(pl.multiple_of(r * kv_rows, kv_rows), kv_rows)
            store_heads(k_s, v_s, rows, *keys_values(
                lat_ref[0, rows, Q_LORA:Q_LORA + KV_LORA], lat_ref[0, rows, Q_LORA + KV_LORA:],
                cos_ref[rows, :], sin_ref[rows, :]))
            return carry

        lax.fori_loop(0, seq // kv_rows, body, 0)

    rows = pl.ds(pl.multiple_of(qi * tq, tq), tq)
    cqn = _rms(lat_ref[0, rows, 0:Q_LORA].astype(F32), gcq_ref[...]).astype(BF16)
    qall = _dot(cqn, wuq_ref[...])
    cos = cos_ref[rows, :]
    sin = sin_ref[rows, :]
    meta_valid = lax.broadcasted_iota(jnp.int32, (tq, META_PAD), 1) < N_META
    for h in range(HEADS):
        qn = qall[:, h * kw:h * kw + NOPE] * scale
        qr = _rope(qall[:, h * kw + NOPE:(h + 1) * kw], cos, sin) * scale
        qh = jnp.concatenate([qn, qr], axis=1).astype(BF16)
        s = _dot_nt(qh, k_s[h])
        sm = jnp.where(meta_valid, _dot_nt(qh, km_s[h]), NEG_BIG)
        m = jnp.maximum(jnp.max(s, axis=-1, keepdims=True), jnp.max(sm, axis=-1, keepdims=True))
        p = jnp.exp2(s - m)
        pm = jnp.exp2(sm - m)
        o = _dot(p.astype(BF16), v_s[h]) + _dot(pm.astype(BF16), vm_s[h])
        a_ref[0, :, h * V_DIM:(h + 1) * V_DIM] = (o[:, 0:V_DIM] / o[:, V_DIM:2 * V_DIM]).astype(BF16)


def _mla(lat, latm, gcq, gckv, wuq, wukv, cos, sin, cosm, sinm, *, tq, kv_rows):
    batch, seq, _ = lat.shape
    kw = NOPE + LANE
    return pl.pallas_call(
        functools.partial(_mla_kernel, tq=tq, kv_rows=kv_rows),
        grid=(batch, seq // tq),
        in_specs=[pl.BlockSpec((1, seq, LAT_WIDTH), lambda b, i: (b, 0, 0))]
        + [_resident(t.shape) for t in (latm, gcq, gckv, wuq, wukv, cos, sin, cosm, sinm)],
        out_specs=pl.BlockSpec((1, tq, HEADS * V_DIM), lambda b, i: (b, i, 0)),
        out_shape=jax.ShapeDtypeStruct((batch, seq, HEADS * V_DIM), BF16),
        scratch_shapes=[pltpu.VMEM((HEADS, seq, kw), BF16), pltpu.VMEM((HEADS, seq, 2 * V_DIM), BF16),
                        pltpu.VMEM((HEADS, META_PAD, kw), BF16), pltpu.VMEM((HEADS, META_PAD, 2 * V_DIM), BF16)],
        compiler_params=pltpu.CompilerParams(
            dimension_semantics=("parallel", "arbitrary"), vmem_limit_bytes=VMEM_LIMIT_BYTES),
        name="mla",
    )(lat, latm, gcq, gckv, wuq, wukv, cos, sin, cosm, sinm)


def _gla_prepare(q, k, cum, reverse):
    qf = q.astype(F32)
    kf = k.astype(F32)
    row = lax.broadcasted_iota(jnp.int32, (CHUNK, HEAD_DIM), 0)
    zero = jnp.zeros((1, HEAD_DIM), F32)
    blocks = []
    for i in range(N_SUB):
        lo, hi = i * SUB, (i + 1) * SUB
        if reverse:
            ref = cum[hi:hi + 1, :] if i < N_SUB - 1 else zero
            valid = row >= lo
        else:
            ref = cum[lo - 1:lo, :] if i > 0 else zero
            valid = row < hi
        q_r = qf[lo:hi] * jnp.exp(cum[lo:hi] - ref)
        k_r = jnp.where(valid, kf * jnp.exp(jnp.where(valid, ref - cum, 0.0)), 0.0)
        blocks.append(_dot_nt(q_r.astype(BF16), k_r.astype(BF16)))
    r2 = lax.broadcasted_iota(jnp.int32, (CHUNK, CHUNK), 0)
    c2 = lax.broadcasted_iota(jnp.int32, (CHUNK, CHUNK), 1)
    keep = (r2 <= c2) if reverse else (r2 >= c2)
    a = jnp.where(keep, jnp.concatenate(blocks, axis=0), 0.0).astype(BF16)
    total = cum[0:1, :] if reverse else cum[CHUNK - 1:CHUNK, :]
    q_b = (qf * jnp.exp(cum)).astype(BF16)
    k_d = (kf * jnp.exp(total - cum)).astype(BF16)
    return a, q_b, k_d, jnp.exp(total)


def _hgrn_kernel(q_ref, kf_ref, kb_ref, bf_ref, cb_ref, v_ref, sg_ref,
                 kfm_ref, bfm_ref, vm_ref, gn_ref, r_ref,
                 acc_ref, st_ref, a_s, qb_s, kd_s, dec_s, *, hg):
    seq = q_ref.shape[1]
    n_chunks = seq // CHUNK
    streams = [(h, d) for h in range(hg) for d in range(2)]

    def rows_of(c, d):
        chunk = c if d == 0 else n_chunks - 1 - c
        return pl.ds(pl.multiple_of(chunk * CHUNK, CHUNK), CHUNK)

    def lanes_of(h):
        return slice(h * HEAD_DIM, (h + 1) * HEAD_DIM)

    for h in range(hg):
        bm = bfm_ref[:, lanes_of(h)]
        k_d = (kfm_ref[:, lanes_of(h)].astype(F32) * jnp.exp(bm[N_META - 1:N_META, :] - bm)).astype(BF16)
        st_ref[0, h] = _dot_tn(vm_ref[:, lanes_of(h)], k_d)
        st_ref[1, h] = jnp.zeros((HEAD_DIM, HEAD_DIM), F32)

    def prepare(c):
        for i, (h, d) in enumerate(streams):
            rows, lanes = rows_of(c, d), lanes_of(h)
            k_ref, cum_ref = (kf_ref, bf_ref) if d == 0 else (kb_ref, cb_ref)
            a, q_b, k_d, dec = _gla_prepare(q_ref[0, rows, lanes], k_ref[0, rows, lanes],
                                            cum_ref[0, rows, lanes], reverse=(d == 1))
            slot = c % PIPE_SLOTS
            a_s[slot, i] = a
            qb_s[slot, i] = q_b
            kd_s[slot, i] = k_d
            dec_s[slot, i] = jnp.broadcast_to(dec, (SUBLANES, HEAD_DIM))

    def apply(c, second_half):
        for i, (h, d) in enumerate(streams):
            rows, lanes = rows_of(c, d), lanes_of(h)
            v = v_ref[0, rows, lanes]
            st = st_ref[d, h]
            slot = c % PIPE_SLOTS
            o = _dot(a_s[slot, i], v) + _dot_nt(qb_s[slot, i], st.astype(BF16))
            st_ref[d, h] = st * dec_s[slot, i, 0:1, :] + _dot_tn(v, kd_s[slot, i])
            if second_half:
                o = _rms(acc_ref[rows, lanes] + o, gn_ref[...])
                r_ref[0, rows, lanes] = (o * sg_ref[0, rows, lanes].astype(F32)).astype(BF16)
            else:
                acc_ref[rows, lanes] = o

    def step(second_half):
        def body(c, carry):
            apply(c, second_half)
            prepare(c + 1)
            return carry
        return body

    prepare(0)
    lax.fori_loop(0, n_chunks // 2, step(False), 0)
    lax.fori_loop(n_chunks // 2, n_chunks - 1, step(True), 0)
    apply(n_chunks - 1, True)


def _hgrn(q, kf, kb, bf, cb, v, sg, kfm, bfm, vm, gn, *, hg):
    batch, seq, width = q.shape
    gw = hg * HEAD_DIM
    blk = pl.BlockSpec((1, seq, gw), lambda b, g: (b, 0, g))
    meta = pl.BlockSpec((N_META, gw), lambda b, g: (0, g))
    return pl.pallas_call(
        functools.partial(_hgrn_kernel, hg=hg),
        grid=(batch, width // gw),
        in_specs=[blk] * 7 + [meta] * 3 + [_resident(gn.shape)],
        out_specs=blk,
        out_shape=jax.ShapeDtypeStruct((batch, seq, width), BF16),
        scratch_shapes=[pltpu.VMEM((seq, gw), F32), pltpu.VMEM((2, hg, HEAD_DIM, HEAD_DIM), F32),
                        pltpu.VMEM((PIPE_SLOTS, 2 * hg, CHUNK, CHUNK), BF16),
                        pltpu.VMEM((PIPE_SLOTS, 2 * hg, CHUNK, HEAD_DIM), BF16),
                        pltpu.VMEM((PIPE_SLOTS, 2 * hg, CHUNK, HEAD_DIM), BF16),
                        pltpu.VMEM((PIPE_SLOTS, 2 * hg, SUBLANES, HEAD_DIM), F32)],
        compiler_params=pltpu.CompilerParams(
            dimension_semantics=("parallel", "parallel"), vmem_limit_bytes=VMEM_LIMIT_BYTES),
        name="hgrn",
    )(q, kf, kb, bf, cb, v, sg, kfm, bfm, vm, gn)


def _post_kernel(x_ref, a_ref, r_ref, ga_ref, gh_ref, wo_ref, gpost_ref, gfpre_ref,
                 wg_ref, wu_ref, wd_ref, gfpost_ref, out_ref):
    mixed = (ga_ref[...].astype(F32) * a_ref[...].astype(F32)
             + gh_ref[...].astype(F32) * r_ref[...].astype(F32)).astype(BF16)
    h1 = x_ref[...] + _rms(_dot(mixed, wo_ref[...]), gpost_ref[...])
    hn = _rms(h1, gfpre_ref[...]).astype(BF16)
    gate = _dot(hn, wg_ref[...])
    act = (gate * _sigmoid(gate) * _dot(hn, wu_ref[...])).astype(BF16)
    out_ref[...] = h1 + _rms(_dot(act, wd_ref[...]), gfpost_ref[...])


def _post(x, a, r, ga, gh, wo, gpost, gfpre, wg, wu, wd, gfpost, *, tm):
    n = x.shape[0]
    tile = pl.BlockSpec((tm, D_MODEL), lambda i: (i, 0))
    return pl.pallas_call(
        _post_kernel,
        grid=(n // tm,),
        in_specs=[tile] * 5 + [_resident(w.shape) for w in (wo, gpost, gfpre, wg, wu, wd, gfpost)],
        out_specs=tile,
        out_shape=jax.ShapeDtypeStruct((n, D_MODEL), F32),
        compiler_params=pltpu.CompilerParams(
            dimension_semantics=("parallel",), vmem_limit_bytes=VMEM_LIMIT_BYTES),
        name="post",
    )(x, a, r, ga, gh, wo, gpost, gfpre, wg, wu, wd, gfpost)


def _rope_lane_map():
    half = ROPE // 2
    return np.concatenate([np.arange(half), LANE // 2 + np.arange(half)])


def _rope_tables(length):
    half = ROPE // 2
    inv = ROPE_THETA ** (-jnp.arange(0, ROPE, 2, dtype=F32) / ROPE)
    ang = jnp.arange(length, dtype=F32)[:, None] * inv[None, :]
    cos, sin = jnp.cos(ang), jnp.sin(ang)
    pad = jnp.zeros((length, LANE // 2 - half), F32)
    cos_t = jnp.concatenate([cos, pad, cos, pad], axis=1)
    sin_t = jnp.concatenate([-sin, pad, sin, pad], axis=1)
    return cos_t, sin_t


def _spread_rope_cols(w):
    out = jnp.zeros((w.shape[0], LANE), w.dtype)
    return out.at[:, _rope_lane_map()].set(w)


def kernel(x, meta_tokens, g_mix_pre, w_in, g_cq, g_ckv, w_uq, w_ukv, lb_logits, g_hgrn, w_o, g_mix_post,
           g_ffn_pre, w_gate, w_up, w_down, g_ffn_post):
    batch, seq, d = x.shape
    depth = w_in.shape[0]
    assert d == D_MODEL and depth == 1 and seq % (2 * CHUNK) == 0
    l = 0
    row2 = lambda g: g.reshape(1, -1).astype(F32)

    in_sizes = [Q_LORA, KV_LORA, ROPE] + [D_MODEL] * N_BIG
    off = np.concatenate([[0], np.cumsum(in_sizes)])
    w = w_in[l]
    wbig = w[:, off[3]:].astype(BF16)
    wlat = jnp.concatenate([w[:, off[0]:off[2]], _spread_rope_cols(w[:, off[2]:off[3]])], axis=1).astype(BF16)
    qk = NOPE + ROPE
    wq = w_uq[l].reshape(Q_LORA, HEADS, qk)
    wq_rope = jnp.zeros((Q_LORA, HEADS, LANE), F32).at[:, :, _rope_lane_map()].set(wq[:, :, NOPE:])
    wuq = jnp.concatenate([wq[:, :, :NOPE], wq_rope], axis=2).reshape(Q_LORA, HEADS * (NOPE + LANE)).astype(BF16)
    wkv = w_ukv[l].reshape(KV_LORA, HEADS, NOPE + V_DIM)
    wukv = jnp.concatenate([wkv[:, :, :NOPE].reshape(KV_LORA, HEADS * NOPE),
                            wkv[:, :, NOPE:].reshape(KV_LORA, HEADS * V_DIM)], axis=1).astype(BF16)
    lb = jax.nn.softmax(lb_logits.astype(F32), axis=0)[l]
    cos_t, sin_t = _rope_tables(N_META + seq)

    proj = functools.partial(_in_proj, gpre=row2(g_mix_pre[l]), wbig=wbig, wlat=wlat, lb=lb)
    xr = x.reshape(batch * seq, d)
    q, kf, kb, bf, cb, v, sg, ga, gh, lat = proj(xr, tm=512, chunk=CHUNK)
    _, kfm, _, bfm, _, vm, _, _, _, latm = proj(meta_tokens.astype(F32), tm=N_META, chunk=N_META)

    b3 = lambda t: t.reshape(batch, seq, t.shape[-1])
    pad_meta = lambda t: jnp.pad(t, ((0, META_PAD - N_META), (0, 0)))
    a = _mla(b3(lat), pad_meta(latm), row2(g_cq[l]), row2(g_ckv[l]), wuq, wukv,
             cos_t[N_META:], sin_t[N_META:], pad_meta(cos_t[:N_META]), pad_meta(sin_t[:N_META]),
             tq=1024, kv_rows=min(512, seq))
    r = _hgrn(b3(q), b3(kf), b3(kb), b3(bf), b3(cb), b3(v), b3(sg), kfm, bfm, vm, row2(g_hgrn[l]),
              hg=4)
    out = _post(xr, a.reshape(batch * seq, d), r.reshape(batch * seq, d), ga, gh,
                w_o[l].astype(BF16), row2(g_mix_post[l]), row2(g_ffn_pre[l]),
                w_gate[l].astype(BF16), w_up[l].astype(BF16), w_down[l].astype(BF16), row2(g_ffn_post[l]),
                tm=512)
    return out.reshape(batch, seq, d)
```

```python
import functools

import numpy as np
import jax
import jax.numpy as jnp
from jax import lax
from jax.experimental import pallas as pl
from jax.experimental.pallas import tpu as pltpu

F32 = jnp.float32
BF16 = jnp.bfloat16

N_META = 16
EPS = 1e-6
D_MODEL = 1024
HEADS = 8
Q_LORA = 384
KV_LORA = 256
NOPE = 128
ROPE = 64
V_DIM = 128
ROPE_THETA = 10000.0
HEAD_DIM = 128
CHUNK = 64
SUB = 16
N_SUB = CHUNK // SUB
PIPE_SLOTS = 2

LANE = 128
SUBLANES = 8
VMEM_LIMIT_BYTES = 56 * 1024 * 1024

IN_PROJ_ROWS = 512
MLA_Q_ROWS = 1024
MLA_KV_ROWS = 512
HGRN_HEADS_PER_STEP = 4
POST_ROWS = 512

LAT_WIDTH = Q_LORA + KV_LORA + LANE
N_BIG = 7
NEG_BIG = -1e30
META_PAD = LANE


def _resident(shape):
    zeros = (0,) * len(shape)
    return pl.BlockSpec(shape, lambda *_: zeros, pipeline_mode=pl.Buffered(1))


def _rms(xf, gain):
    return xf * lax.rsqrt(jnp.mean(xf * xf, axis=-1, keepdims=True) + EPS) * gain


def _sigmoid(z):
    return 1.0 / (1.0 + jnp.exp(-z))


def _dot(a, b):
    return jnp.dot(a, b, preferred_element_type=F32)


def _dot_nt(a, b):
    return lax.dot_general(a, b, (((1,), (1,)), ((), ())), preferred_element_type=F32)


def _dot_tn(a, b):
    return lax.dot_general(a, b, (((0,), (0,)), ((), ())), preferred_element_type=F32)


def _chunk_cumsum(g, chunk, reverse):
    rows, width = g.shape
    x = g.reshape(rows // SUBLANES, SUBLANES, width)
    pos = lax.broadcasted_iota(jnp.int32, x.shape, 1)
    step = 1
    while step < SUBLANES:
        if reverse:
            shifted = pltpu.roll(x, SUBLANES - step, axis=1)
            ok = pos < SUBLANES - step
        else:
            shifted = pltpu.roll(x, step, axis=1)
            ok = pos >= step
        x = x + jnp.where(ok, shifted, 0.0)
        step *= 2
    groups = chunk // SUBLANES
    x = x.reshape(rows // chunk, groups, SUBLANES, width)
    out = [None] * groups
    carry = None
    for j in (range(groups - 1, -1, -1) if reverse else range(groups)):
        cur = x[:, j] if carry is None else x[:, j] + carry
        out[j] = cur
        carry = cur[:, 0:1, :] if reverse else cur[:, SUBLANES - 1:SUBLANES, :]
    return jnp.stack(out, axis=1).reshape(rows, width)


def _in_proj_kernel(x_ref, gpre_ref, wbig_ref, wlat_ref, lb_ref,
                    q_ref, kf_ref, kb_ref, bf_ref, cb_ref, v_ref, sg_ref, ga_ref, gh_ref, lat_ref,
                    *, chunk):
    hn = _rms(x_ref[...], gpre_ref[...]).astype(BF16)

    proj = _dot(hn, wbig_ref[...])

    def seg(j):
        return proj[:, j * D_MODEL:(j + 1) * D_MODEL]

    z = seg(0)
    q_ref[...] = (z * _sigmoid(z)).astype(BF16)
    for d, (k_ref, s_ref) in enumerate(((kf_ref, bf_ref), (kb_ref, cb_ref))):
        lb = lb_ref[d:d + 1, :]
        f = lb + (1.0 - lb) * _sigmoid(seg(1 + d))
        k_ref[...] = (1.0 - f).astype(BF16)
        s_ref[...] = _chunk_cumsum(jnp.log(f), chunk, reverse=(d == 1))
    v_ref[...] = seg(3).astype(BF16)
    z = seg(4)
    sg_ref[...] = (z * _sigmoid(z)).astype(BF16)
    ga_ref[...] = _sigmoid(seg(5)).astype(BF16)
    gh_ref[...] = _sigmoid(seg(6)).astype(BF16)
    lat_ref[...] = _dot(hn, wlat_ref[...]).astype(BF16)


def _in_proj(rows, gpre, wbig, wlat, lb, *, tm, chunk):
    n = rows.shape[0]
    tile = lambda w: pl.BlockSpec((tm, w), lambda i: (i, 0))
    wide = lambda dt: jax.ShapeDtypeStruct((n, D_MODEL), dt)
    out_dtypes = (BF16, BF16, BF16, F32, F32, BF16, BF16, BF16, BF16)
    return pl.pallas_call(
        functools.partial(_in_proj_kernel, chunk=chunk),
        grid=(n // tm,),
        in_specs=[tile(D_MODEL), _resident(gpre.shape), _resident(wbig.shape),
                  _resident(wlat.shape), _resident(lb.shape)],
        out_specs=[tile(D_MODEL)] * 9 + [tile(LAT_WIDTH)],
        out_shape=[wide(dt) for dt in out_dtypes] + [jax.ShapeDtypeStruct((n, LAT_WIDTH), BF16)],
        compiler_params=pltpu.CompilerParams(
            dimension_semantics=("parallel",), vmem_limit_bytes=VMEM_LIMIT_BYTES),
        name="in_proj",
    )(rows, gpre, wbig, wlat, lb)


def _rope(xr, cos, sin):
    return xr * cos + pltpu.roll(xr, LANE // 2, axis=1) * sin


def _mla_kernel(lat_ref, latm_ref, gcq_ref, gckv_ref, wuq_ref, wukv_ref,
                cos_ref, sin_ref, cosm_ref, sinm_ref,
                a_ref, k_s, v_s, km_s, vm_s, *, tq, kv_rows):
    qi = pl.program_id(1)
    seq = lat_ref.shape[1]
    scale = (NOPE + ROPE) ** -0.5 * np.log2(np.e)
    kw = NOPE + LANE

    def keys_values(ckv, kr_raw, cos, sin):
        ckvn = _rms(ckv.astype(F32), gckv_ref[...]).astype(BF16)
        kv = _dot(ckvn, wukv_ref[...]).astype(BF16)
        kr = _rope(kr_raw.astype(F32), cos, sin).astype(BF16)
        return kv, kr

    def store_heads(kd_s, vd_s, rows, kv, kr):
        for h in range(HEADS):
            kd_s[h, rows, 0:NOPE] = kv[:, h * NOPE:(h + 1) * NOPE]
            kd_s[h, rows, NOPE:kw] = kr
            vd_s[h, rows, 0:V_DIM] = kv[:, HEADS * NOPE + h * V_DIM:HEADS * NOPE + (h + 1) * V_DIM]
            vd_s[h, rows, V_DIM:2 * V_DIM] = jnp.ones((kv.shape[0], V_DIM), BF16)

    @pl.when(qi == 0)
    def _():
        store_heads(km_s, vm_s, slice(None), *keys_values(
            latm_ref[:, Q_LORA:Q_LORA + KV_LORA], latm_ref[:, Q_LORA + KV_LORA:], cosm_ref[...], sinm_ref[...]))

        def body(r, carry):
            rows = pl.ds(pl.multiple_of(r * kv_rows, kv_rows), kv_rows)
            store_heads(k_s, v_s, rows, *keys_values(
                lat_ref[0, rows, Q_LORA:Q_LORA + KV_LORA], lat_ref[0, rows, Q_LORA + KV_LORA:],
                cos_ref[rows, :], sin_ref[rows, :]))
            return carry

        lax.fori_loop(0, seq // kv_rows, body, 0)

    rows = pl.ds(pl.multiple_of(qi * tq, tq), tq)
    cqn = _rms(lat_ref[0, rows, 0:Q_LORA].astype(F32), gcq_ref[...]).astype(BF16)
    qall = _dot(cqn, wuq_ref[...])
    cos = cos_ref[rows, :]
    sin = sin_ref[rows, :]
    meta_valid = lax.broadcasted_iota(jnp.int32, (tq, META_PAD), 1) < N_META
    for h in range(HEADS):
        qn = qall[:, h * kw:h * kw + NOPE] * scale
        qr = _rope(qall[:, h * kw + NOPE:(h + 1) * kw], cos, sin) * scale
        qh = jnp.concatenate([qn, qr], axis=1).astype(BF16)
        s = _dot_nt(qh, k_s[h])
        sm = jnp.where(meta_valid, _dot_nt(qh, km_s[h]), NEG_BIG)
        m = jnp.maximum(jnp.max(s, axis=-1, keepdims=True), jnp.max(sm, axis=-1, keepdims=True))
        p = jnp.exp2(s - m)
        pm = jnp.exp2(sm - m)
        o = _dot(p.astype(BF16), v_s[h]) + _dot(pm.astype(BF16), vm_s[h])
        a_ref[0, :, h * V_DIM:(h + 1) * V_DIM] = (o[:, 0:V_DIM] / o[:, V_DIM:2 * V_DIM]).astype(BF16)


def _mla(lat, latm, gcq, gckv, wuq, wukv, cos, sin, cosm, sinm, *, tq, kv_rows):
    batch, seq, _ = lat.shape
    kw = NOPE + LANE
    return pl.pallas_call(
        functools.partial(_mla_kernel, tq=tq, kv_rows=kv_rows),
        grid=(batch, seq // tq),
        in_specs=[pl.BlockSpec((1, seq, LAT_WIDTH), lambda b, i: (b, 0, 0))]
        + [_resident(t.shape) for t in (latm, gcq, gckv, wuq, wukv, cos, sin, cosm, sinm)],
        out_specs=pl.BlockSpec((1, tq, HEADS * V_DIM), lambda b, i: (b, i, 0)),
        out_shape=jax.ShapeDtypeStruct((batch, seq, HEADS * V_DIM), BF16),
        scratch_shapes=[pltpu.VMEM((HEADS, seq, kw), BF16), pltpu.VMEM((HEADS, seq, 2 * V_DIM), BF16),
                        pltpu.VMEM((HEADS, META_PAD, kw), BF16), pltpu.VMEM((HEADS, META_PAD, 2 * V_DIM), BF16)],
        compiler_params=pltpu.CompilerParams(
            dimension_semantics=("parallel", "arbitrary"), vmem_limit_bytes=VMEM_LIMIT_BYTES),
        name="mla",
    )(lat, latm, gcq, gckv, wuq, wukv, cos, sin, cosm, sinm)


def _gla_prepare(q, k, cum, reverse):
    qf = q.astype(F32)
    kf = k.astype(F32)
    row = lax.broadcasted_iota(jnp.int32, (CHUNK, HEAD_DIM), 0)
    zero = jnp.zeros((1, HEAD_DIM), F32)
    blocks = []
    for i in range(N_SUB):
        lo, hi = i * SUB, (i + 1) * SUB
        if reverse:
            ref = cum[hi:hi + 1, :] if i < N_SUB - 1 else zero
            valid = row >= lo
        else:
            ref = cum[lo - 1:lo, :] if i > 0 else zero
            valid = row < hi
        q_r = qf[lo:hi] * jnp.exp(cum[lo:hi] - ref)
        k_r = jnp.where(valid, kf * jnp.exp(jnp.where(valid, ref - cum, 0.0)), 0.0)
        blocks.append(_dot_nt(q_r.astype(BF16), k_r.astype(BF16)))
    r2 = lax.broadcasted_iota(jnp.int32, (CHUNK, CHUNK), 0)
    c2 = lax.broadcasted_iota(jnp.int32, (CHUNK, CHUNK), 1)
    keep = (r2 <= c2) if reverse else (r2 >= c2)
    a = jnp.where(keep, jnp.concatenate(blocks, axis=0), 0.0).astype(BF16)
    total = cum[0:1, :] if reverse else cum[CHUNK - 1:CHUNK, :]
    q_b = (qf * jnp.exp(cum)).astype(BF16)
    k_d = (kf * jnp.exp(total - cum)).astype(BF16)
    return a, q_b, k_d, jnp.exp(total)


def _hgrn_kernel(q_ref, kf_ref, kb_ref, bf_ref, cb_ref, v_ref, sg_ref,
                 kfm_ref, bfm_ref, vm_ref, gn_ref, r_ref,
                 acc_ref, st_ref, a_s, qb_s, kd_s, dec_s, *, hg):
    seq = q_ref.shape[1]
    n_chunks = seq // CHUNK
    streams = [(h, d) for h in range(hg) for d in range(2)]

    def rows_of(c, d):
        chunk = c if d == 0 else n_chunks - 1 - c
        return pl.ds(pl.multiple_of(chunk * CHUNK, CHUNK), CHUNK)

    def lanes_of(h):
        return slice(h * HEAD_DIM, (h + 1) * HEAD_DIM)

    for h in range(hg):
        bm = bfm_ref[:, lanes_of(h)]
        k_d = (kfm_ref[:, lanes_of(h)].astype(F32) * jnp.exp(bm[N_META - 1:N_META, :] - bm)).astype(BF16)
        st_ref[0, h] = _dot_tn(vm_ref[:, lanes_of(h)], k_d)
        st_ref[1, h] = jnp.zeros((HEAD_DIM, HEAD_DIM), F32)

    def prepare(c):
        for i, (h, d) in enumerate(streams):
            rows, lanes = rows_of(c, d), lanes_of(h)
            k_ref, cum_ref = (kf_ref, bf_ref) if d == 0 else (kb_ref, cb_ref)
            a, q_b, k_d, dec = _gla_prepare(q_ref[0, rows, lanes], k_ref[0, rows, lanes],
                                            cum_ref[0, rows, lanes], reverse=(d == 1))
            slot = c % PIPE_SLOTS
            a_s[slot, i] = a
            qb_s[slot, i] = q_b
            kd_s[slot, i] = k_d
            dec_s[slot, i] = jnp.broadcast_to(dec, (SUBLANES, HEAD_DIM))

    def apply(c, second_half):
        for i, (h, d) in enumerate(streams):
            rows, lanes = rows_of(c, d), lanes_of(h)
            v = v_ref[0, rows, lanes]
            st = st_ref[d, h]
            slot = c % PIPE_SLOTS
            o = _dot(a_s[slot, i], v) + _dot_nt(st.astype(BF16), qb_s[slot, i]).T
            st_ref[d, h] = st * dec_s[slot, i, 0:1, :] + _dot_tn(v, kd_s[slot, i])
            if second_half:
                o = _rms(acc_ref[rows, lanes] + o, gn_ref[...])
                r_ref[0, rows, lanes] = (o * sg_ref[0, rows, lanes].astype(F32)).astype(BF16)
            else:
                acc_ref[rows, lanes] = o

    def step(second_half):
        def body(c, carry):
            apply(c, second_half)
            prepare(c + 1)
            return carry
        return body

    prepare(0)
    lax.fori_loop(0, n_chunks // 2, step(False), 0)
    lax.fori_loop(n_chunks // 2, n_chunks - 1, step(True), 0)
    apply(n_chunks - 1, True)


def _hgrn(q, kf, kb, bf, cb, v, sg, kfm, bfm, vm, gn, *, hg):
    batch, seq, width = q.shape
    gw = hg * HEAD_DIM
    blk = pl.BlockSpec((1, seq, gw), lambda b, g: (b, 0, g))
    meta = pl.BlockSpec((N_META, gw), lambda b, g: (0, g))
    return pl.pallas_call(
        functools.partial(_hgrn_kernel, hg=hg),
        grid=(batch, width // gw),
        in_specs=[blk] * 7 + [meta] * 3 + [_resident(gn.shape)],
        out_specs=blk,
        out_shape=jax.ShapeDtypeStruct((batch, seq, width), BF16),
        scratch_shapes=[pltpu.VMEM((seq, gw), F32), pltpu.VMEM((2, hg, HEAD_DIM, HEAD_DIM), F32),
                        pltpu.VMEM((PIPE_SLOTS, 2 * hg, CHUNK, CHUNK), BF16),
                        pltpu.VMEM((PIPE_SLOTS, 2 * hg, CHUNK, HEAD_DIM), BF16),
                        pltpu.VMEM((PIPE_SLOTS, 2 * hg, CHUNK, HEAD_DIM), BF16),
                        pltpu.VMEM((PIPE_SLOTS, 2 * hg, SUBLANES, HEAD_DIM), F32)],
        compiler_params=pltpu.CompilerParams(
            dimension_semantics=("parallel", "parallel"), vmem_limit_bytes=VMEM_LIMIT_BYTES),
        name="hgrn",
    )(q, kf, kb, bf, cb, v, sg, kfm, bfm, vm, gn)


def _post_kernel(x_ref, a_ref, r_ref, ga_ref, gh_ref, wo_ref, gpost_ref, gfpre_ref,
                 wg_ref, wu_ref, wd_ref, gfpost_ref, out_ref):
    mixed = (ga_ref[...].astype(F32) * a_ref[...].astype(F32)
             + gh_ref[...].astype(F32) * r_ref[...].astype(F32)).astype(BF16)
    h1 = x_ref[...] + _rms(_dot(mixed, wo_ref[...]), gpost_ref[...])
    hn = _rms(h1, gfpre_ref[...]).astype(BF16)
    gate = _dot(hn, wg_ref[...])
    act = (gate * _sigmoid(gate) * _dot(hn, wu_ref[...])).astype(BF16)
    out_ref[...] = h1 + _rms(_dot(act, wd_ref[...]), gfpost_ref[...])


def _post(x, a, r, ga, gh, wo, gpost, gfpre, wg, wu, wd, gfpost, *, tm):
    n = x.shape[0]
    tile = pl.BlockSpec((tm, D_MODEL), lambda i: (i, 0))
    return pl.pallas_call(
        _post_kernel,
        grid=(n // tm,),
        in_specs=[tile] * 5 + [_resident(w.shape) for w in (wo, gpost, gfpre, wg, wu, wd, gfpost)],
        out_specs=tile,
        out_shape=jax.ShapeDtypeStruct((n, D_MODEL), F32),
        compiler_params=pltpu.CompilerParams(
            dimension_semantics=("parallel",), vmem_limit_bytes=VMEM_LIMIT_BYTES),
        name="post",
    )(x, a, r, ga, gh, wo, gpost, gfpre, wg, wu, wd, gfpost)


def _rope_lane_map():
    half = ROPE // 2
    return np.concatenate([np.arange(half), LANE // 2 + np.arange(half)])


def _rope_tables(length):
    half = ROPE // 2
    inv = ROPE_THETA ** (-jnp.arange(0, ROPE, 2, dtype=F32) / ROPE)
    ang = jnp.arange(length, dtype=F32)[:, None] * inv[None, :]
    cos, sin = jnp.cos(ang), jnp.sin(ang)
    pad = jnp.zeros((length, LANE // 2 - half), F32)
    cos_t = jnp.concatenate([cos, pad, cos, pad], axis=1)
    sin_t = jnp.concatenate([-sin, pad, sin, pad], axis=1)
    return cos_t, sin_t


def _spread_rope_cols(w):
    out = jnp.zeros((w.shape[0], LANE), w.dtype)
    return out.at[:, _rope_lane_map()].set(w)


def kernel(x, meta_tokens, g_mix_pre, w_in, g_cq, g_ckv, w_uq, w_ukv, lb_logits, g_hgrn, w_o, g_mix_post,
           g_ffn_pre, w_gate, w_up, w_down, g_ffn_post):
    batch, seq, d = x.shape
    depth = w_in.shape[0]
    assert d == D_MODEL and depth == 1 and seq % (2 * CHUNK) == 0
    assert seq % MLA_Q_ROWS == 0 and seq % MLA_KV_ROWS == 0 and IN_PROJ_ROWS % CHUNK == 0
    assert (batch * seq) % IN_PROJ_ROWS == 0 and (batch * seq) % POST_ROWS == 0
    l = 0
    row2 = lambda g: g.reshape(1, -1).astype(F32)

    in_sizes = [Q_LORA, KV_LORA, ROPE] + [D_MODEL] * N_BIG
    off = np.concatenate([[0], np.cumsum(in_sizes)])
    w = w_in[l]
    wbig = w[:, off[3]:].astype(BF16)
    wlat = jnp.concatenate([w[:, off[0]:off[2]], _spread_rope_cols(w[:, off[2]:off[3]])], axis=1).astype(BF16)
    qk = NOPE + ROPE
    wq = w_uq[l].reshape(Q_LORA, HEADS, qk)
    wq_rope = jnp.zeros((Q_LORA, HEADS, LANE), F32).at[:, :, _rope_lane_map()].set(wq[:, :, NOPE:])
    wuq = jnp.concatenate([wq[:, :, :NOPE], wq_rope], axis=2).reshape(Q_LORA, HEADS * (NOPE + LANE)).astype(BF16)
    wkv = w_ukv[l].reshape(KV_LORA, HEADS, NOPE + V_DIM)
    wukv = jnp.concatenate([wkv[:, :, :NOPE].reshape(KV_LORA, HEADS * NOPE),
                            wkv[:, :, NOPE:].reshape(KV_LORA, HEADS * V_DIM)], axis=1).astype(BF16)
    lb = jax.nn.softmax(lb_logits.astype(F32), axis=0)[l]
    cos_t, sin_t = _rope_tables(N_META + seq)

    proj = functools.partial(_in_proj, gpre=row2(g_mix_pre[l]), wbig=wbig, wlat=wlat, lb=lb)
    xr = x.reshape(batch * seq, d)
    q, kf, kb, bf, cb, v, sg, ga, gh, lat = proj(xr, tm=IN_PROJ_ROWS, chunk=CHUNK)
    _, kfm, _, bfm, _, vm, _, _, _, latm = proj(meta_tokens.astype(F32), tm=N_META, chunk=N_META)

    b3 = lambda t: t.reshape(batch, seq, t.shape[-1])
    pad_meta = lambda t: jnp.pad(t, ((0, META_PAD - N_META), (0, 0)))
    a = _mla(b3(lat), pad_meta(latm), row2(g_cq[l]), row2(g_ckv[l]), wuq, wukv,
             cos_t[N_META:], sin_t[N_META:], pad_meta(cos_t[:N_META]), pad_meta(sin_t[:N_META]),
             tq=MLA_Q_ROWS, kv_rows=MLA_KV_ROWS)
    r = _hgrn(b3(q), b3(kf), b3(kb), b3(bf), b3(cb), b3(v), b3(sg), kfm, bfm, vm, row2(g_hgrn[l]),
              hg=HGRN_HEADS_PER_STEP)
    out = _post(xr, a.reshape(batch * seq, d), r.reshape(batch * seq, d), ga, gh,
                w_o[l].astype(BF16), row2(g_mix_post[l]), row2(g_ffn_pre[l]),
                w_gate[l].astype(BF16), w_up[l].astype(BF16), w_down[l].astype(BF16), row2(g_ffn_post[l]),
                tm=POST_ROWS)
    return out.reshape(batch, seq, d)
```

```python
import functools

import numpy as np
import jax
import jax.numpy as jnp
from jax import lax
from jax.experimental import pallas as pl
from jax.experimental.pallas import tpu as pltpu

F32 = jnp.float32
BF16 = jnp.bfloat16

N_META = 16
EPS = 1e-6
D_MODEL = 1024
HEADS = 8
Q_LORA = 384
KV_LORA = 256
NOPE = 128
ROPE = 64
V_DIM = 128
ROPE_THETA = 10000.0
HEAD_DIM = 128
CHUNK = 64
SUB = 16
N_SUB = CHUNK // SUB
PIPE_SLOTS = 2

LANE = 128
SUBLANES = 8
VMEM_LIMIT_BYTES = 56 * 1024 * 1024

IN_PROJ_ROWS = 512
MLA_Q_ROWS = 1024
MLA_KV_ROWS = 512
HGRN_HEADS_PER_STEP = 4
POST_ROWS = 512

LAT_WIDTH = Q_LORA + KV_LORA + LANE
N_BIG = 7
NEG_BIG = -1e30
META_PAD = LANE


def _resident(shape):
    zeros = (0,) * len(shape)
    return pl.BlockSpec(shape, lambda *_: zeros, pipeline_mode=pl.Buffered(1))


def _rms(xf, gain):
    return xf * lax.rsqrt(jnp.mean(xf * xf, axis=-1, keepdims=True) + EPS) * gain


def _sigmoid(z):
    return 1.0 / (1.0 + jnp.exp(-z))


def _dot(a, b):
    return jnp.dot(a, b, preferred_element_type=F32)


def _dot_nt(a, b):
    return lax.dot_general(a, b, (((1,), (1,)), ((), ())), preferred_element_type=F32)


def _dot_tn(a, b):
    return lax.dot_general(a, b, (((0,), (0,)), ((), ())), preferred_element_type=F32)


def _chunk_cumsum(g, chunk, reverse):
    rows, width = g.shape
    x = g.reshape(rows // SUBLANES, SUBLANES, width)
    pos = lax.broadcasted_iota(jnp.int32, x.shape, 1)
    step = 1
    while step < SUBLANES:
        if reverse:
            shifted = pltpu.roll(x, SUBLANES - step, axis=1)
            ok = pos < SUBLANES - step
        else:
            shifted = pltpu.roll(x, step, axis=1)
            ok = pos >= step
        x = x + jnp.where(ok, shifted, 0.0)
        step *= 2
    groups = chunk // SUBLANES
    x = x.reshape(rows // chunk, groups, SUBLANES, width)
    out = [None] * groups
    carry = None
    for j in (range(groups - 1, -1, -1) if reverse else range(groups)):
        cur = x[:, j] if carry is None else x[:, j] + carry
        out[j] = cur
        carry = cur[:, 0:1, :] if reverse else cur[:, SUBLANES - 1:SUBLANES, :]
    return jnp.stack(out, axis=1).reshape(rows, width)


def _in_proj_kernel(x_ref, gpre_ref, wbig_ref, wlat_ref, lb_ref,
                    q_ref, kf_ref, kb_ref, bf_ref, cb_ref, v_ref, sg_ref, ga_ref, gh_ref, lat_ref,
                    *, chunk):
    hn = _rms(x_ref[...], gpre_ref[...]).astype(BF16)

    proj = _dot(hn, wbig_ref[...])

    def seg(j):
        return proj[:, j * D_MODEL:(j + 1) * D_MODEL]

    z = seg(0)
    q_ref[...] = (z * _sigmoid(z)).astype(BF16)
    for d, (k_ref, s_ref) in enumerate(((kf_ref, bf_ref), (kb_ref, cb_ref))):
        lb = lb_ref[d:d + 1, :]
        f = lb + (1.0 - lb) * _sigmoid(seg(1 + d))
        k_ref[...] = (1.0 - f).astype(BF16)
        s_ref[...] = _chunk_cumsum(jnp.log(f), chunk, reverse=(d == 1))
    v_ref[...] = seg(3).astype(BF16)
    z = seg(4)
    sg_ref[...] = (z * _sigmoid(z)).astype(BF16)
    ga_ref[...] = _sigmoid(seg(5)).astype(BF16)
    gh_ref[...] = _sigmoid(seg(6)).astype(BF16)
    lat_ref[...] = _dot(hn, wlat_ref[...]).astype(BF16)


def _in_proj(rows, gpre, wbig, wlat, lb, *, tm, chunk):
    n = rows.shape[0]
    tile = lambda w: pl.BlockSpec((tm, w), lambda i: (i, 0))
    wide = lambda dt: jax.ShapeDtypeStruct((n, D_MODEL), dt)
    out_dtypes = (BF16, BF16, BF16, F32, F32, BF16, BF16, BF16, BF16)
    return pl.pallas_call(
        functools.partial(_in_proj_kernel, chunk=chunk),
        grid=(n // tm,),
        in_specs=[tile(D_MODEL), _resident(gpre.shape), _resident(wbig.shape),
                  _resident(wlat.shape), _resident(lb.shape)],
        out_specs=[tile(D_MODEL)] * 9 + [tile(LAT_WIDTH)],
        out_shape=[wide(dt) for dt in out_dtypes] + [jax.ShapeDtypeStruct((n, LAT_WIDTH), BF16)],
        compiler_params=pltpu.CompilerParams(
            dimension_semantics=("parallel",), vmem_limit_bytes=VMEM_LIMIT_BYTES),
        name="in_proj",
    )(rows, gpre, wbig, wlat, lb)


def _rope(xr, cos, sin):
    return xr * cos + pltpu.roll(xr, LANE // 2, axis=1) * sin


def _mla_kernel(lat_ref, latm_ref, gcq_ref, gckv_ref, wuq_ref, wukv_ref,
                cos_ref, sin_ref, cosm_ref, sinm_ref,
                a_ref, k_s, v_s, km_s, vm_s, *, tq, kv_rows):
    qi = pl.program_id(1)
    seq = lat_ref.shape[1]
    scale = (NOPE + ROPE) ** -0.5 * np.log2(np.e)
    kw = NOPE + LANE

    def keys_values(ckv, kr_raw, cos, sin):
        ckvn = _rms(ckv.astype(F32), gckv_ref[...]).astype(BF16)
        kv = _dot(ckvn, wukv_ref[...]).astype(BF16)
        kr = _rope(kr_raw.astype(F32), cos, sin).astype(BF16)
        return kv, kr

    def store_heads(kd_s, vd_s, rows, kv, kr):
        for h in range(HEADS):
            kd_s[h, rows, 0:NOPE] = kv[:, h * NOPE:(h + 1) * NOPE]
            kd_s[h, rows, NOPE:kw] = kr
            vd_s[h, rows, 0:V_DIM] = kv[:, HEADS * NOPE + h * V_DIM:HEADS * NOPE + (h + 1) * V_DIM]
            vd_s[h, rows, V_DIM:2 * V_DIM] = jnp.ones((kv.shape[0], V_DIM), BF16)

    @pl.when(qi == 0)
    def _():
        store_heads(km_s, vm_s, slice(None), *keys_values(
            latm_ref[:, Q_LORA:Q_LORA + KV_LORA], latm_ref[:, Q_LORA + KV_LORA:], cosm_ref[...], sinm_ref[...]))

        def body(r, carry):
            rows = pl.ds(pl.multiple_of(r * kv_rows, kv_rows), kv_rows)
            store_heads(k_s, v_s, rows, *keys_values(
                lat_ref[0, rows, Q_LORA:Q_LORA + KV_LORA], lat_ref[0, rows, Q_LORA + KV_LORA:],
                cos_ref[rows, :], sin_ref[rows, :]))
            return carry

        lax.fori_loop(0, seq // kv_rows, body, 0)

    rows = pl.ds(pl.multiple_of(qi * tq, tq), tq)
    cqn = _rms(lat_ref[0, rows, 0:Q_LORA].astype(F32), gcq_ref[...]).astype(BF16)
    qall = _dot(cqn, wuq_ref[...])
    cos = cos_ref[rows, :]
    sin = sin_ref[rows, :]
    meta_valid = lax.broadcasted_iota(jnp.int32, (tq, META_PAD), 1) < N_META
    for h in range(HEADS):
        qn = qall[:, h * kw:h * kw + NOPE] * scale
        qr = _rope(qall[:, h * kw + NOPE:(h + 1) * kw], cos, sin) * scale
        qh = jnp.concatenate([qn, qr], axis=1).astype(BF16)
        s = _dot_nt(qh, k_s[h])
        sm = jnp.where(meta_valid, _dot_nt(qh, km_s[h]), NEG_BIG)
        m = jnp.maximum(jnp.max(s, axis=-1, keepdims=True), jnp.max(sm, axis=-1, keepdims=True))
        p = jnp.exp2(s - m)
        pm = jnp.exp2(sm - m)
        o = _dot(p.astype(BF16), v_s[h]) + _dot(pm.astype(BF16), vm_s[h])
        a_ref[0, :, h * V_DIM:(h + 1) * V_DIM] = (o[:, 0:V_DIM] / o[:, V_DIM:2 * V_DIM]).astype(BF16)


def _mla(lat, latm, gcq, gckv, wuq, wukv, cos, sin, cosm, sinm, *, tq, kv_rows):
    batch, seq, _ = lat.shape
    kw = NOPE + LANE
    return pl.pallas_call(
        functools.partial(_mla_kernel, tq=tq, kv_rows=kv_rows),
        grid=(batch, seq // tq),
        in_specs=[pl.BlockSpec((1, seq, LAT_WIDTH), lambda b, i: (b, 0, 0))]
        + [_resident(t.shape) for t in (latm, gcq, gckv, wuq, wukv, cos, sin, cosm, sinm)],
        out_specs=pl.BlockSpec((1, tq, HEADS * V_DIM), lambda b, i: (b, i, 0)),
        out_shape=jax.ShapeDtypeStruct((batch, seq, HEADS * V_DIM), BF16),
        scratch_shapes=[pltpu.VMEM((HEADS, seq, kw), BF16), pltpu.VMEM((HEADS, seq, 2 * V_DIM), BF16),
                        pltpu.VMEM((HEADS, META_PAD, kw), BF16), pltpu.VMEM((HEADS, META_PAD, 2 * V_DIM), BF16)],
        compiler_params=pltpu.CompilerParams(
            dimension_semantics=("parallel", "arbitrary"), vmem_limit_bytes=VMEM_LIMIT_BYTES),
        name="mla",
    )(lat, latm, gcq, gckv, wuq, wukv, cos, sin, cosm, sinm)


def _gla_prepare(q, k, cum, reverse):
    qf = q.astype(F32)
    kf = k.astype(F32)
    row = lax.broadcasted_iota(jnp.int32, (CHUNK, HEAD_DIM), 0)
    zero = jnp.zeros((1, HEAD_DIM), F32)
    blocks = []
    for i in range(N_SUB):
        lo, hi = i * SUB, (i + 1) * SUB
        if reverse:
            ref = cum[hi:hi + 1, :] if i < N_SUB - 1 else zero
            valid = row >= lo
        else:
            ref = cum[lo - 1:lo, :] if i > 0 else zero
            valid = row < hi
        q_r = qf[lo:hi] * jnp.exp(cum[lo:hi] - ref)
        k_r = jnp.where(valid, kf * jnp.exp(jnp.where(valid, ref - cum, 0.0)), 0.0)
        blocks.append(_dot_nt(q_r.astype(BF16), k_r.astype(BF16)))
    r2 = lax.broadcasted_iota(jnp.int32, (CHUNK, CHUNK), 0)
    c2 = lax.broadcasted_iota(jnp.int32, (CHUNK, CHUNK), 1)
    keep = (r2 <= c2) if reverse else (r2 >= c2)
    a = jnp.where(keep, jnp.concatenate(blocks, axis=0), 0.0).astype(BF16)
    total = cum[0:1, :] if reverse else cum[CHUNK - 1:CHUNK, :]
    q_b = (qf * jnp.exp(cum)).astype(BF16)
    k_d = (kf * jnp.exp(total - cum)).astype(BF16)
    return a, q_b, k_d, jnp.exp(total)


def _hgrn_kernel(q_ref, kf_ref, kb_ref, bf_ref, cb_ref, v_ref, sg_ref,
                 kfm_ref, bfm_ref, vm_ref, gn_ref, r_ref,
                 acc_ref, st_ref, a_s, qb_s, kd_s, dec_s, *, hg):
    seq = q_ref.shape[1]
    n_chunks = seq // CHUNK
    streams = [(h, d) for h in range(hg) for d in range(2)]

    def rows_of(c, d):
        chunk = c if d == 0 else n_chunks - 1 - c
        return pl.ds(pl.multiple_of(chunk * CHUNK, CHUNK), CHUNK)

    def lanes_of(h):
        return slice(h * HEAD_DIM, (h + 1) * HEAD_DIM)

    for h in range(hg):
        bm = bfm_ref[:, lanes_of(h)]
        k_d = (kfm_ref[:, lanes_of(h)].astype(F32) * jnp.exp(bm[N_META - 1:N_META, :] - bm)).astype(BF16)
        st_ref[0, h] = _dot_tn(vm_ref[:, lanes_of(h)], k_d)
        st_ref[1, h] = jnp.zeros((HEAD_DIM, HEAD_DIM), F32)

    def prepare(c):
        for i, (h, d) in enumerate(streams):
            rows, lanes = rows_of(c, d), lanes_of(h)
            k_ref, cum_ref = (kf_ref, bf_ref) if d == 0 else (kb_ref, cb_ref)
            a, q_b, k_d, dec = _gla_prepare(q_ref[0, rows, lanes], k_ref[0, rows, lanes],
                                            cum_ref[0, rows, lanes], reverse=(d == 1))
            slot = c % PIPE_SLOTS
            a_s[slot, i] = a
            qb_s[slot, i] = q_b
            kd_s[slot, i] = k_d
            dec_s[slot, i] = jnp.broadcast_to(dec, (SUBLANES, HEAD_DIM))

    def apply(c, second_half):
        for i, (h, d) in enumerate(streams):
            rows, lanes = rows_of(c, d), lanes_of(h)
            v = v_ref[0, rows, lanes]
            st = st_ref[d, h]
            slot = c % PIPE_SLOTS
            o = _dot(a_s[slot, i], v) + _dot_nt(st.astype(BF16), qb_s[slot, i]).T
            st_ref[d, h] = st * dec_s[slot, i, 0:1, :] + _dot_tn(v, kd_s[slot, i])
            if second_half:
                o = _rms(acc_ref[rows, lanes] + o, gn_ref[...])
                r_ref[0, rows, lanes] = (o * sg_ref[0, rows, lanes].astype(F32)).astype(BF16)
            else:
                acc_ref[rows, lanes] = o

    def steps(first, count, second_half):
        def pair(i, carry):
            for c in (first + 2 * i, first + 2 * i + 1):
                apply(c, second_half)
                prepare(c + 1)
            return carry
        lax.fori_loop(0, count // 2, pair, 0)
        if count % 2:
            apply(first + count - 1, second_half)
            prepare(first + count)

    half = n_chunks // 2
    prepare(0)
    steps(0, half, False)
    steps(half, half - 1, True)
    apply(n_chunks - 1, True)


def _hgrn(q, kf, kb, bf, cb, v, sg, kfm, bfm, vm, gn, *, hg):
    batch, seq, width = q.shape
    gw = hg * HEAD_DIM
    blk = pl.BlockSpec((1, seq, gw), lambda b, g: (b, 0, g))
    meta = pl.BlockSpec((N_META, gw), lambda b, g: (0, g))
    return pl.pallas_call(
        functools.partial(_hgrn_kernel, hg=hg),
        grid=(batch, width // gw),
        in_specs=[blk] * 7 + [meta] * 3 + [_resident(gn.shape)],
        out_specs=blk,
        out_shape=jax.ShapeDtypeStruct((batch, seq, width), BF16),
        scratch_shapes=[pltpu.VMEM((seq, gw), F32), pltpu.VMEM((2, hg, HEAD_DIM, HEAD_DIM), F32),
                        pltpu.VMEM((PIPE_SLOTS, 2 * hg, CHUNK, CHUNK), BF16),
                        pltpu.VMEM((PIPE_SLOTS, 2 * hg, CHUNK, HEAD_DIM), BF16),
                        pltpu.VMEM((PIPE_SLOTS, 2 * hg, CHUNK, HEAD_DIM), BF16),
                        pltpu.VMEM((PIPE_SLOTS, 2 * hg, SUBLANES, HEAD_DIM), F32)],
        compiler_params=pltpu.CompilerParams(
            dimension_semantics=("parallel", "parallel"), vmem_limit_bytes=VMEM_LIMIT_BYTES),
        name="hgrn",
    )(q, kf, kb, bf, cb, v, sg, kfm, bfm, vm, gn)


def _post_kernel(x_ref, a_ref, r_ref, ga_ref, gh_ref, wo_ref, gpost_ref, gfpre_ref,
                 wg_ref, wu_ref, wd_ref, gfpost_ref, out_ref):
    mixed = (ga_ref[...].astype(F32) * a_ref[...].astype(F32)
             + gh_ref[...].astype(F32) * r_ref[...].astype(F32)).astype(BF16)
    h1 = x_ref[...] + _rms(_dot(mixed, wo_ref[...]), gpost_ref[...])
    hn = _rms(h1, gfpre_ref[...]).astype(BF16)
    gate = _dot(hn, wg_ref[...])
    act = (gate * _sigmoid(gate) * _dot(hn, wu_ref[...])).astype(BF16)
    out_ref[...] = h1 + _rms(_dot(act, wd_ref[...]), gfpost_ref[...])


def _post(x, a, r, ga, gh, wo, gpost, gfpre, wg, wu, wd, gfpost, *, tm):
    n = x.shape[0]
    tile = pl.BlockSpec((tm, D_MODEL), lambda i: (i, 0))
    return pl.pallas_call(
        _post_kernel,
        grid=(n // tm,),
        in_specs=[tile] * 5 + [_resident(w.shape) for w in (wo, gpost, gfpre, wg, wu, wd, gfpost)],
        out_specs=tile,
        out_shape=jax.ShapeDtypeStruct((n, D_MODEL), F32),
        compiler_params=pltpu.CompilerParams(
            dimension_semantics=("parallel",), vmem_limit_bytes=VMEM_LIMIT_BYTES),
        name="post",
    )(x, a, r, ga, gh, wo, gpost, gfpre, wg, wu, wd, gfpost)


def _rope_lane_map():
    half = ROPE // 2
    return np.concatenate([np.arange(half), LANE // 2 + np.arange(half)])


def _rope_tables(length):
    half = ROPE // 2
    inv = ROPE_THETA ** (-jnp.arange(0, ROPE, 2, dtype=F32) / ROPE)
    ang = jnp.arange(length, dtype=F32)[:, None] * inv[None, :]
    cos, sin = jnp.cos(ang), jnp.sin(ang)
    pad = jnp.zeros((length, LANE // 2 - half), F32)
    cos_t = jnp.concatenate([cos, pad, cos, pad], axis=1)
    sin_t = jnp.concatenate([-sin, pad, sin, pad], axis=1)
    return cos_t, sin_t


def _spread_rope_cols(w):
    out = jnp.zeros((w.shape[0], LANE), w.dtype)
    return out.at[:, _rope_lane_map()].set(w)


def kernel(x, meta_tokens, g_mix_pre, w_in, g_cq, g_ckv, w_uq, w_ukv, lb_logits, g_hgrn, w_o, g_mix_post,
           g_ffn_pre, w_gate, w_up, w_down, g_ffn_post):
    batch, seq, d = x.shape
    depth = w_in.shape[0]
    assert d == D_MODEL and depth == 1 and seq % (2 * CHUNK) == 0
    assert seq % MLA_Q_ROWS == 0 and seq % MLA_KV_ROWS == 0 and IN_PROJ_ROWS % CHUNK == 0
    assert (batch * seq) % IN_PROJ_ROWS == 0 and (batch * seq) % POST_ROWS == 0
    l = 0
    row2 = lambda g: g.reshape(1, -1).astype(F32)

    in_sizes = [Q_LORA, KV_LORA, ROPE] + [D_MODEL] * N_BIG
    off = np.concatenate([[0], np.cumsum(in_sizes)])
    w = w_in[l]
    wbig = w[:, off[3]:].astype(BF16)
    wlat = jnp.concatenate([w[:, off[0]:off[2]], _spread_rope_cols(w[:, off[2]:off[3]])], axis=1).astype(BF16)
    qk = NOPE + ROPE
    wq = w_uq[l].reshape(Q_LORA, HEADS, qk)
    wq_rope = jnp.zeros((Q_LORA, HEADS, LANE), F32).at[:, :, _rope_lane_map()].set(wq[:, :, NOPE:])
    wuq = jnp.concatenate([wq[:, :, :NOPE], wq_rope], axis=2).reshape(Q_LORA, HEADS * (NOPE + LANE)).astype(BF16)
    wkv = w_ukv[l].reshape(KV_LORA, HEADS, NOPE + V_DIM)
    wukv = jnp.concatenate([wkv[:, :, :NOPE].reshape(KV_LORA, HEADS * NOPE),
                            wkv[:, :, NOPE:].reshape(KV_LORA, HEADS * V_DIM)], axis=1).astype(BF16)
    lb = jax.nn.softmax(lb_logits.astype(F32), axis=0)[l]
    cos_t, sin_t = _rope_tables(N_META + seq)

    proj = functools.partial(_in_proj, gpre=row2(g_mix_pre[l]), wbig=wbig, wlat=wlat, lb=lb)
    xr = x.reshape(batch * seq, d)
    q, kf, kb, bf, cb, v, sg, ga, gh, lat = proj(xr, tm=IN_PROJ_ROWS, chunk=CHUNK)
    _, kfm, _, bfm, _, vm, _, _, _, latm = proj(meta_tokens.astype(F32), tm=N_META, chunk=N_META)

    b3 = lambda t: t.reshape(batch, seq, t.shape[-1])
    pad_meta = lambda t: jnp.pad(t, ((0, META_PAD - N_META), (0, 0)))
    a = _mla(b3(lat), pad_meta(latm), row2(g_cq[l]), row2(g_ckv[l]), wuq, wukv,
             cos_t[N_META:], sin_t[N_META:], pad_meta(cos_t[:N_META]), pad_meta(sin_t[:N_META]),
             tq=MLA_Q_ROWS, kv_rows=MLA_KV_ROWS)
    r = _hgrn(b3(q), b3(kf), b3(kb), b3(bf), b3(cb), b3(v), b3(sg), kfm, bfm, vm, row2(g_hgrn[l]),
              hg=HGRN_HEADS_PER_STEP)
    out = _post(xr, a.reshape(batch * seq, d), r.reshape(batch * seq, d), ga, gh,
                w_o[l].astype(BF16), row2(g_mix_post[l]), row2(g_ffn_pre[l]),
                w_gate[l].astype(BF16), w_up[l].astype(BF16), w_down[l].astype(BF16), row2(g_ffn_post[l]),
                tm=POST_ROWS)
    return out.reshape(batch, seq, d)
```

```python
import functools

import numpy as np
import jax
import jax.numpy as jnp
from jax import lax
from jax.experimental import pallas as pl
from jax.experimental.pallas import tpu as pltpu

F32 = jnp.float32
BF16 = jnp.bfloat16

N_META = 16
EPS = 1e-6
D_MODEL = 1024
HEADS = 8
Q_LORA = 384
KV_LORA = 256
NOPE = 128
ROPE = 64
V_DIM = 128
ROPE_THETA = 10000.0
HEAD_DIM = 128
CHUNK = 64
SUB = 16
N_SUB = CHUNK // SUB
PIPE_SLOTS = 2
STEPS_PER_TRIP = 4

LANE = 128
SUBLANES = 8
VMEM_LIMIT_BYTES = 56 * 1024 * 1024

IN_PROJ_ROWS = 512
MLA_Q_ROWS = 1024
MLA_KV_ROWS = 512
HGRN_HEADS_PER_STEP = 4
POST_ROWS = 512

LAT_WIDTH = Q_LORA + KV_LORA + LANE
N_BIG = 7
NEG_BIG = -1e30
META_PAD = LANE


def _resident(shape):
    zeros = (0,) * len(shape)
    return pl.BlockSpec(shape, lambda *_: zeros, pipeline_mode=pl.Buffered(1))


def _rms(xf, gain):
    return xf * lax.rsqrt(jnp.mean(xf * xf, axis=-1, keepdims=True) + EPS) * gain


def _sigmoid(z):
    return 1.0 / (1.0 + jnp.exp(-z))


def _dot(a, b):
    return jnp.dot(a, b, preferred_element_type=F32)


def _dot_nt(a, b):
    return lax.dot_general(a, b, (((1,), (1,)), ((), ())), preferred_element_type=F32)


def _dot_tn(a, b):
    return lax.dot_general(a, b, (((0,), (0,)), ((), ())), preferred_element_type=F32)


def _chunk_cumsum(g, chunk, reverse):
    rows, width = g.shape
    x = g.reshape(rows // SUBLANES, SUBLANES, width)
    pos = lax.broadcasted_iota(jnp.int32, x.shape, 1)
    step = 1
    while step < SUBLANES:
        if reverse:
            shifted = pltpu.roll(x, SUBLANES - step, axis=1)
            ok = pos < SUBLANES - step
        else:
            shifted = pltpu.roll(x, step, axis=1)
            ok = pos >= step
        x = x + jnp.where(ok, shifted, 0.0)
        step *= 2
    groups = chunk // SUBLANES
    x = x.reshape(rows // chunk, groups, SUBLANES, width)
    out = [None] * groups
    carry = None
    for j in (range(groups - 1, -1, -1) if reverse else range(groups)):
        cur = x[:, j] if carry is None else x[:, j] + carry
        out[j] = cur
        carry = cur[:, 0:1, :] if reverse else cur[:, SUBLANES - 1:SUBLANES, :]
    return jnp.stack(out, axis=1).reshape(rows, width)


def _in_proj_kernel(x_ref, gpre_ref, wbig_ref, wlat_ref, lb_ref,
                    q_ref, kf_ref, kb_ref, bf_ref, cb_ref, v_ref, sg_ref, ga_ref, gh_ref, lat_ref,
                    *, chunk):
    hn = _rms(x_ref[...], gpre_ref[...]).astype(BF16)

    proj = _dot(hn, wbig_ref[...])

    def seg(j):
        return proj[:, j * D_MODEL:(j + 1) * D_MODEL]

    z = seg(0)
    q_ref[...] = (z * _sigmoid(z)).astype(BF16)
    for d, (k_ref, s_ref) in enumerate(((kf_ref, bf_ref), (kb_ref, cb_ref))):
        lb = lb_ref[d:d + 1, :]
        f = lb + (1.0 - lb) * _sigmoid(seg(1 + d))
        k_ref[...] = (1.0 - f).astype(BF16)
        s_ref[...] = _chunk_cumsum(jnp.log(f), chunk, reverse=(d == 1))
    v_ref[...] = seg(3).astype(BF16)
    z = seg(4)
    sg_ref[...] = (z * _sigmoid(z)).astype(BF16)
    ga_ref[...] = _sigmoid(seg(5)).astype(BF16)
    gh_ref[...] = _sigmoid(seg(6)).astype(BF16)
    lat_ref[...] = _dot(hn, wlat_ref[...]).astype(BF16)


def _in_proj(rows, gpre, wbig, wlat, lb, *, tm, chunk):
    n = rows.shape[0]
    tile = lambda w: pl.BlockSpec((tm, w), lambda i: (i, 0))
    wide = lambda dt: jax.ShapeDtypeStruct((n, D_MODEL), dt)
    out_dtypes = (BF16, BF16, BF16, F32, F32, BF16, BF16, BF16, BF16)
    return pl.pallas_call(
        functools.partial(_in_proj_kernel, chunk=chunk),
        grid=(n // tm,),
        in_specs=[tile(D_MODEL), _resident(gpre.shape), _resident(wbig.shape),
                  _resident(wlat.shape), _resident(lb.shape)],
        out_specs=[tile(D_MODEL)] * 9 + [tile(LAT_WIDTH)],
        out_shape=[wide(dt) for dt in out_dtypes] + [jax.ShapeDtypeStruct((n, LAT_WIDTH), BF16)],
        compiler_params=pltpu.CompilerParams(
            dimension_semantics=("parallel",), vmem_limit_bytes=VMEM_LIMIT_BYTES),
        name="in_proj",
    )(rows, gpre, wbig, wlat, lb)


def _rope(xr, cos, sin):
    return xr * cos + pltpu.roll(xr, LANE // 2, axis=1) * sin


def _mla_kernel(lat_ref, latm_ref, gcq_ref, gckv_ref, wuq_ref, wukv_ref,
                cos_ref, sin_ref, cosm_ref, sinm_ref,
                a_ref, k_s, v_s, km_s, vm_s, *, tq, kv_rows):
    qi = pl.program_id(1)
    seq = lat_ref.shape[1]
    scale = (NOPE + ROPE) ** -0.5 * np.log2(np.e)
    kw = NOPE + LANE

    def keys_values(ckv, kr_raw, cos, sin):
        ckvn = _rms(ckv.astype(F32), gckv_ref[...]).astype(BF16)
        kv = _dot(ckvn, wukv_ref[...]).astype(BF16)
        kr = _rope(kr_raw.astype(F32), cos, sin).astype(BF16)
        return kv, kr

    def store_heads(kd_s, vd_s, rows, kv, kr):
        for h in range(HEADS):
            kd_s[h, rows, 0:NOPE] = kv[:, h * NOPE:(h + 1) * NOPE]
            kd_s[h, rows, NOPE:kw] = kr
            vd_s[h, rows, 0:V_DIM] = kv[:, HEADS * NOPE + h * V_DIM:HEADS * NOPE + (h + 1) * V_DIM]
            vd_s[h, rows, V_DIM:2 * V_DIM] = jnp.ones((kv.shape[0], V_DIM), BF16)

    @pl.when(qi == 0)
    def _():
        store_heads(km_s, vm_s, slice(None), *keys_values(
            latm_ref[:, Q_LORA:Q_LORA + KV_LORA], latm_ref[:, Q_LORA + KV_LORA:], cosm_ref[...], sinm_ref[...]))

        def body(r, carry):
            rows = pl.ds(pl.multiple_of(r * kv_rows, kv_rows), kv_rows)
            store_heads(k_s, v_s, rows, *keys_values(
                lat_ref[0, rows, Q_LORA:Q_LORA + KV_LORA], lat_ref[0, rows, Q_LORA + KV_LORA:],
                cos_ref[rows, :], sin_ref[rows, :]))
            return carry

        lax.fori_loop(0, seq // kv_rows, body, 0)

    rows = pl.ds(pl.multiple_of(qi * tq, tq), tq)
    cqn = _rms(lat_ref[0, rows, 0:Q_LORA].astype(F32), gcq_ref[...]).astype(BF16)
    qall = _dot(cqn, wuq_ref[...])
    cos = cos_ref[rows, :]
    sin = sin_ref[rows, :]
    meta_valid = lax.broadcasted_iota(jnp.int32, (tq, META_PAD), 1) < N_META
    for h in range(HEADS):
        qn = qall[:, h * kw:h * kw + NOPE] * scale
        qr = _rope(qall[:, h * kw + NOPE:(h + 1) * kw], cos, sin) * scale
        qh = jnp.concatenate([qn, qr], axis=1).astype(BF16)
        s = _dot_nt(qh, k_s[h])
        sm = jnp.where(meta_valid, _dot_nt(qh, km_s[h]), NEG_BIG)
        m = jnp.maximum(jnp.max(s, axis=-1, keepdims=True), jnp.max(sm, axis=-1, keepdims=True))
        p = jnp.exp2(s - m)
        pm = jnp.exp2(sm - m)
        o = _dot(p.astype(BF16), v_s[h]) + _dot(pm.astype(BF16), vm_s[h])
        a_ref[0, :, h * V_DIM:(h + 1) * V_DIM] = (o[:, 0:V_DIM] / o[:, V_DIM:2 * V_DIM]).astype(BF16)


def _mla(lat, latm, gcq, gckv, wuq, wukv, cos, sin, cosm, sinm, *, tq, kv_rows):
    batch, seq, _ = lat.shape
    kw = NOPE + LANE
    return pl.pallas_call(
        functools.partial(_mla_kernel, tq=tq, kv_rows=kv_rows),
        grid=(batch, seq // tq),
        in_specs=[pl.BlockSpec((1, seq, LAT_WIDTH), lambda b, i: (b, 0, 0))]
        + [_resident(t.shape) for t in (latm, gcq, gckv, wuq, wukv, cos, sin, cosm, sinm)],
        out_specs=pl.BlockSpec((1, tq, HEADS * V_DIM), lambda b, i: (b, i, 0)),
        out_shape=jax.ShapeDtypeStruct((batch, seq, HEADS * V_DIM), BF16),
        scratch_shapes=[pltpu.VMEM((HEADS, seq, kw), BF16), pltpu.VMEM((HEADS, seq, 2 * V_DIM), BF16),
                        pltpu.VMEM((HEADS, META_PAD, kw), BF16), pltpu.VMEM((HEADS, META_PAD, 2 * V_DIM), BF16)],
        compiler_params=pltpu.CompilerParams(
            dimension_semantics=("parallel", "arbitrary"), vmem_limit_bytes=VMEM_LIMIT_BYTES),
        name="mla",
    )(lat, latm, gcq, gckv, wuq, wukv, cos, sin, cosm, sinm)


def _gla_prepare(q, k, cum, reverse):
    qf = q.astype(F32)
    kf = k.astype(F32)
    row = lax.broadcasted_iota(jnp.int32, (CHUNK, HEAD_DIM), 0)
    zero = jnp.zeros((1, HEAD_DIM), F32)
    blocks = []
    for i in range(N_SUB):
        lo, hi = i * SUB, (i + 1) * SUB
        if reverse:
            ref = cum[hi:hi + 1, :] if i < N_SUB - 1 else zero
            valid = row >= lo
        else:
            ref = cum[lo - 1:lo, :] if i > 0 else zero
            valid = row < hi
        q_r = qf[lo:hi] * jnp.exp(cum[lo:hi] - ref)
        k_r = jnp.where(valid, kf * jnp.exp(jnp.where(valid, ref - cum, 0.0)), 0.0)
        blocks.append(_dot_nt(q_r.astype(BF16), k_r.astype(BF16)))
    r2 = lax.broadcasted_iota(jnp.int32, (CHUNK, CHUNK), 0)
    c2 = lax.broadcasted_iota(jnp.int32, (CHUNK, CHUNK), 1)
    keep = (r2 <= c2) if reverse else (r2 >= c2)
    a = jnp.where(keep, jnp.concatenate(blocks, axis=0), 0.0).astype(BF16)
    total = cum[0:1, :] if reverse else cum[CHUNK - 1:CHUNK, :]
    q_b = (qf * jnp.exp(cum)).astype(BF16)
    k_d = (kf * jnp.exp(total - cum)).astype(BF16)
    return a, q_b, k_d, jnp.exp(total)


def _hgrn_kernel(q_ref, kf_ref, kb_ref, bf_ref, cb_ref, v_ref, sg_ref,
                 kfm_ref, bfm_ref, vm_ref, gn_ref, r_ref,
                 acc_ref, st_ref, a_s, qb_s, kd_s, dec_s, *, hg):
    seq = q_ref.shape[1]
    n_chunks = seq // CHUNK
    streams = [(h, d) for h in range(hg) for d in range(2)]

    def rows_of(c, d):
        chunk = c if d == 0 else n_chunks - 1 - c
        return pl.ds(pl.multiple_of(chunk * CHUNK, CHUNK), CHUNK)

    def lanes_of(h):
        return slice(h * HEAD_DIM, (h + 1) * HEAD_DIM)

    for h in range(hg):
        bm = bfm_ref[:, lanes_of(h)]
        k_d = (kfm_ref[:, lanes_of(h)].astype(F32) * jnp.exp(bm[N_META - 1:N_META, :] - bm)).astype(BF16)
        st_ref[0, h] = _dot_tn(vm_ref[:, lanes_of(h)], k_d)
        st_ref[1, h] = jnp.zeros((HEAD_DIM, HEAD_DIM), F32)

    def prepare(c):
        for i, (h, d) in enumerate(streams):
            rows, lanes = rows_of(c, d), lanes_of(h)
            k_ref, cum_ref = (kf_ref, bf_ref) if d == 0 else (kb_ref, cb_ref)
            a, q_b, k_d, dec = _gla_prepare(q_ref[0, rows, lanes], k_ref[0, rows, lanes],
                                            cum_ref[0, rows, lanes], reverse=(d == 1))
            slot = c % PIPE_SLOTS
            a_s[slot, i] = a
            qb_s[slot, i] = q_b
            kd_s[slot, i] = k_d
            dec_s[slot, i] = jnp.broadcast_to(dec, (SUBLANES, HEAD_DIM))

    def apply(c, second_half):
        for i, (h, d) in enumerate(streams):
            rows, lanes = rows_of(c, d), lanes_of(h)
            v = v_ref[0, rows, lanes]
            st = st_ref[d, h]
            slot = c % PIPE_SLOTS
            o = _dot(a_s[slot, i], v) + _dot_nt(st.astype(BF16), qb_s[slot, i]).T
            st_ref[d, h] = st * dec_s[slot, i, 0:1, :] + _dot_tn(v, kd_s[slot, i])
            if second_half:
                o = _rms(acc_ref[rows, lanes] + o, gn_ref[...])
                r_ref[0, rows, lanes] = (o * sg_ref[0, rows, lanes].astype(F32)).astype(BF16)
            else:
                acc_ref[rows, lanes] = o

    def steps(first, count, second_half):
        def trip(i, carry):
            for j in range(STEPS_PER_TRIP):
                apply(first + STEPS_PER_TRIP * i + j, second_half)
                prepare(first + STEPS_PER_TRIP * i + j + 1)
            return carry
        lax.fori_loop(0, count // STEPS_PER_TRIP, trip, 0)
        for c in range(first + count - count % STEPS_PER_TRIP, first + count):
            apply(c, second_half)
            prepare(c + 1)

    half = n_chunks // 2
    prepare(0)
    steps(0, half, False)
    steps(half, half - 1, True)
    apply(n_chunks - 1, True)


def _hgrn(q, kf, kb, bf, cb, v, sg, kfm, bfm, vm, gn, *, hg):
    batch, seq, width = q.shape
    gw = hg * HEAD_DIM
    blk = pl.BlockSpec((1, seq, gw), lambda b, g: (b, 0, g))
    meta = pl.BlockSpec((N_META, gw), lambda b, g: (0, g))
    return pl.pallas_call(
        functools.partial(_hgrn_kernel, hg=hg),
        grid=(batch, width // gw),
        in_specs=[blk] * 7 + [meta] * 3 + [_resident(gn.shape)],
        out_specs=blk,
        out_shape=jax.ShapeDtypeStruct((batch, seq, width), BF16),
        scratch_shapes=[pltpu.VMEM((seq, gw), F32), pltpu.VMEM((2, hg, HEAD_DIM, HEAD_DIM), F32),
                        pltpu.VMEM((PIPE_SLOTS, 2 * hg, CHUNK, CHUNK), BF16),
                        pltpu.VMEM((PIPE_SLOTS, 2 * hg, CHUNK, HEAD_DIM), BF16),
                        pltpu.VMEM((PIPE_SLOTS, 2 * hg, CHUNK, HEAD_DIM), BF16),
                        pltpu.VMEM((PIPE_SLOTS, 2 * hg, SUBLANES, HEAD_DIM), F32)],
        compiler_params=pltpu.CompilerParams(
            dimension_semantics=("parallel", "parallel"), vmem_limit_bytes=VMEM_LIMIT_BYTES),
        name="hgrn",
    )(q, kf, kb, bf, cb, v, sg, kfm, bfm, vm, gn)


def _post_kernel(x_ref, a_ref, r_ref, ga_ref, gh_ref, wo_ref, gpost_ref, gfpre_ref,
                 wg_ref, wu_ref, wd_ref, gfpost_ref, out_ref):
    mixed = (ga_ref[...].astype(F32) * a_ref[...].astype(F32)
             + gh_ref[...].astype(F32) * r_ref[...].astype(F32)).astype(BF16)
    h1 = x_ref[...] + _rms(_dot(mixed, wo_ref[...]), gpost_ref[...])
    hn = _rms(h1, gfpre_ref[...]).astype(BF16)
    gate = _dot(hn, wg_ref[...])
    act = (gate * _sigmoid(gate) * _dot(hn, wu_ref[...])).astype(BF16)
    out_ref[...] = h1 + _rms(_dot(act, wd_ref[...]), gfpost_ref[...])


def _post(x, a, r, ga, gh, wo, gpost, gfpre, wg, wu, wd, gfpost, *, tm):
    n = x.shape[0]
    tile = pl.BlockSpec((tm, D_MODEL), lambda i: (i, 0))
    return pl.pallas_call(
        _post_kernel,
        grid=(n // tm,),
        in_specs=[tile] * 5 + [_resident(w.shape) for w in (wo, gpost, gfpre, wg, wu, wd, gfpost)],
        out_specs=tile,
        out_shape=jax.ShapeDtypeStruct((n, D_MODEL), F32),
        compiler_params=pltpu.CompilerParams(
            dimension_semantics=("parallel",), vmem_limit_bytes=VMEM_LIMIT_BYTES),
        name="post",
    )(x, a, r, ga, gh, wo, gpost, gfpre, wg, wu, wd, gfpost)


def _rope_lane_map():
    half = ROPE // 2
    return np.concatenate([np.arange(half), LANE // 2 + np.arange(half)])


def _rope_tables(length):
    half = ROPE // 2
    inv = ROPE_THETA ** (-jnp.arange(0, ROPE, 2, dtype=F32) / ROPE)
    ang = jnp.arange(length, dtype=F32)[:, None] * inv[None, :]
    cos, sin = jnp.cos(ang), jnp.sin(ang)
    pad = jnp.zeros((length, LANE // 2 - half), F32)
    cos_t = jnp.concatenate([cos, pad, cos, pad], axis=1)
    sin_t = jnp.concatenate([-sin, pad, sin, pad], axis=1)
    return cos_t, sin_t


def _spread_rope_cols(w):
    out = jnp.zeros((w.shape[0], LANE), w.dtype)
    return out.at[:, _rope_lane_map()].set(w)


def kernel(x, meta_tokens, g_mix_pre, w_in, g_cq, g_ckv, w_uq, w_ukv, lb_logits, g_hgrn, w_o, g_mix_post,
           g_ffn_pre, w_gate, w_up, w_down, g_ffn_post):
    batch, seq, d = x.shape
    depth = w_in.shape[0]
    assert d == D_MODEL and depth == 1 and seq % (2 * CHUNK) == 0
    assert seq % MLA_Q_ROWS == 0 and seq % MLA_KV_ROWS == 0 and IN_PROJ_ROWS % CHUNK == 0
    assert (batch * seq) % IN_PROJ_ROWS == 0 and (batch * seq) % POST_ROWS == 0
    l = 0
    row2 = lambda g: g.reshape(1, -1).astype(F32)

    in_sizes = [Q_LORA, KV_LORA, ROPE] + [D_MODEL] * N_BIG
    off = np.concatenate([[0], np.cumsum(in_sizes)])
    w = w_in[l]
    wbig = w[:, off[3]:].astype(BF16)
    wlat = jnp.concatenate([w[:, off[0]:off[2]], _spread_rope_cols(w[:, off[2]:off[3]])], axis=1).astype(BF16)
    qk = NOPE + ROPE
    wq = w_uq[l].reshape(Q_LORA, HEADS, qk)
    wq_rope = jnp.zeros((Q_LORA, HEADS, LANE), F32).at[:, :, _rope_lane_map()].set(wq[:, :, NOPE:])
    wuq = jnp.concatenate([wq[:, :, :NOPE], wq_rope], axis=2).reshape(Q_LORA, HEADS * (NOPE + LANE)).astype(BF16)
    wkv = w_ukv[l].reshape(KV_LORA, HEADS, NOPE + V_DIM)
    wukv = jnp.concatenate([wkv[:, :, :NOPE].reshape(KV_LORA, HEADS * NOPE),
                            wkv[:, :, NOPE:].reshape(KV_LORA, HEADS * V_DIM)], axis=1).astype(BF16)
    lb = jax.nn.softmax(lb_logits.astype(F32), axis=0)[l]
    cos_t, sin_t = _rope_tables(N_META + seq)

    proj = functools.partial(_in_proj, gpre=row2(g_mix_pre[l]), wbig=wbig, wlat=wlat, lb=lb)
    xr = x.reshape(batch * seq, d)
    q, kf, kb, bf, cb, v, sg, ga, gh, lat = proj(xr, tm=IN_PROJ_ROWS, chunk=CHUNK)
    _, kfm, _, bfm, _, vm, _, _, _, latm = proj(meta_tokens.astype(F32), tm=N_META, chunk=N_META)

    b3 = lambda t: t.reshape(batch, seq, t.shape[-1])
    pad_meta = lambda t: jnp.pad(t, ((0, META_PAD - N_META), (0, 0)))
    a = _mla(b3(lat), pad_meta(latm), row2(g_cq[l]), row2(g_ckv[l]), wuq, wukv,
             cos_t[N_META:], sin_t[N_META:], pad_meta(cos_t[:N_META]), pad_meta(sin_t[:N_META]),
             tq=MLA_Q_ROWS, kv_rows=MLA_KV_ROWS)
    r = _hgrn(b3(q), b3(kf), b3(kb), b3(bf), b3(cb), b3(v), b3(sg), kfm, bfm, vm, row2(g_hgrn[l]),
              hg=HGRN_HEADS_PER_STEP)
    out = _post(xr, a.reshape(batch * seq, d), r.reshape(batch * seq, d), ga, gh,
                w_o[l].astype(BF16), row2(g_mix_post[l]), row2(g_ffn_pre[l]),
                w_gate[l].astype(BF16), w_up[l].astype(BF16), w_down[l].astype(BF16), row2(g_ffn_post[l]),
                tm=POST_ROWS)
    return out.reshape(batch, seq, d)
```
